```python
import math
import jax
import jax.numpy as jnp
from jax import lax
import numpy as np

D_MODEL = 2048
BATCH = 2
SEQ = 4096
DEPTH = 4
DEC_BATCH = 8
DEC_SEQ = 8
PAST_LEN = 16384
PAGE_SIZE = 128

C_CONV = 512
CONV_WIDTH = 31
H_DIFF = 6
DK_DIFF = 64
DV_DIFF = 2 * DK_DIFF
H_FOX = 6
D_FOX = 128
D_MIX = C_CONV + H_DIFF * DV_DIFF + H_FOX * D_FOX
D_FF = ((8 * D_MODEL + 3 * 256 - 1) // (3 * 256)) * 256
ROPE_THETA = 10000.0
Q_BLOCK = 128
EPS = 1e-6
NEG_INF = -1e30
IN_SIZES = (C_CONV, C_CONV,
            H_DIFF * 2 * DK_DIFF, H_DIFF * 2 * DK_DIFF, H_DIFF * DV_DIFF,
            H_FOX * D_FOX, H_FOX * D_FOX, H_FOX * D_FOX, H_FOX)
N_IN = 2 * C_CONV + H_DIFF * (4 * DK_DIFF + DV_DIFF) + H_FOX * (3 * D_FOX + 1)

kernel_name = "hymba_conv_diff_fox_decoder_step"


def _rmsnorm(x, g):
    xf = x.astype(jnp.float32)
    y = xf * lax.rsqrt(jnp.mean(xf * xf, axis=-1, keepdims=True) + EPS)
    return (y * g).astype(x.dtype)


def _layernorm(x, g, b):
    xf = x.astype(jnp.float32)
    mu = jnp.mean(xf, axis=-1, keepdims=True)
    var = jnp.mean(jnp.square(xf - mu), axis=-1, keepdims=True)
    return ((xf - mu) * lax.rsqrt(var + EPS) * g + b).astype(x.dtype)


def _rope(x, pos):
    T, d = x.shape[1], x.shape[-1]
    half = d // 2
    inv = ROPE_THETA ** (-jnp.arange(half, dtype=jnp.float32) / half)
    ang = pos.astype(jnp.float32)[:, None] * inv[None, :]
    shape = (1, T) + (1,) * (x.ndim - 3) + (half,)
    cos, sin = jnp.cos(ang).reshape(shape), jnp.sin(ang).reshape(shape)
    x1, x2 = x[..., :half].astype(jnp.float32), x[..., half:].astype(jnp.float32)
    return jnp.concatenate([x1 * cos - x2 * sin, x2 * cos + x1 * sin], axis=-1).astype(x.dtype)


def _forget_prefix(lf):
    return lf - lax.cumsum(lf, axis=1, reverse=True)


def _joint_softmax(logit_segs):
    sizes = [s.shape[-1] for s in logit_segs]
    p = jax.nn.softmax(jnp.concatenate(logit_segs, axis=-1), axis=-1)
    return jnp.split(p, [int(i) for i in np.cumsum(sizes)[:-1]], axis=-1)


def _sweep(fn, T):
    blk = math.gcd(T, Q_BLOCK)
    starts = jnp.arange(T // blk, dtype=jnp.int32) * blk
    out = lax.map(lambda s0: fn(s0, blk), starts)
    out = jnp.moveaxis(out, 0, 1)
    return out.reshape((out.shape[0], T) + out.shape[3:])


def _diff_attend(q, q_pos, segs, lam):
    scale = DK_DIFF ** -0.5
    logits = []
    for k, v, k_pos in segs:
        s = jnp.einsum("bqhmd,bkhmd->bmhqk", q, k).astype(jnp.float32) * scale
        mask = k_pos[None, :] <= q_pos[:, None]
        logits.append(jnp.where(mask, s, NEG_INF))
    probs = _joint_softmax(logits)
    out = 0.0
    for p, (k, v, k_pos) in zip(probs, segs):
        w = p[:, 0] - lam * p[:, 1]
        out = out + jnp.einsum("bhqk,bkhd->bqhd", w.astype(v.dtype), v,
                               preferred_element_type=jnp.float32)
    return out


def _fox_attend(q, fq, q_pos, segs):
    scale = D_FOX ** -0.5
    logits = []
    for k, v, fk, k_pos in segs:
        s = jnp.einsum("bqhd,bkhd->bhqk", q, k).astype(jnp.float32) * scale
        decay = jnp.swapaxes(fq, 1, 2)[..., :, None] - jnp.swapaxes(fk, 1, 2)[..., None, :]
        mask = k_pos[None, :] <= q_pos[:, None]
        logits.append(jnp.where(mask, s + decay, NEG_INF))
    probs = _joint_softmax(logits)
    out = 0.0
    for p, (k, v, fk, k_pos) in zip(probs, segs):
        out = out + jnp.einsum("bhqk,bkhd->bqhd", p.astype(v.dtype), v,
                               preferred_element_type=jnp.float32)
    return out


def _gather_pages(pool, page_table):
    g = pool[page_table]
    return g.reshape((g.shape[0], g.shape[1] * g.shape[2]) + g.shape[3:])


def _layer(x, c, pos, past, conv_buf, lam_init, lp):
    B, T, _ = x.shape
    ada = jax.nn.silu(c) @ lp["w_ada"] + lp["b_ada"]
    sh1, sc1, g1, sh2, sc2, g2 = [a[:, None, :] for a in jnp.split(ada, 6, axis=-1)]
    h = _rmsnorm(x, lp["norm_mix"]) * (1 + sc1) + sh1
    splits = [int(s) for s in np.cumsum(IN_SIZES)[:-1]]
    z_cv, z_cg, z_dq, z_dk, z_dv, z_fq, z_fk, z_fv, z_ff = jnp.split(h @ lp["w_in"], splits, axis=-1)

    u = z_cv * jax.nn.sigmoid(z_cg)
    u_pad = jnp.concatenate([conv_buf.astype(u.dtype), u], axis=1)
    yc = lax.conv_general_dilated(u_pad, lp["conv_w"][:, None, :].astype(u.dtype), (1,), "VALID",
                                  dimension_numbers=("NWC", "WIO", "NWC"),
                                  feature_group_count=C_CONV) + lp["conv_b"]
    yc = jax.nn.silu(_layernorm(yc, lp["conv_ln_g"], lp["conv_ln_b"]))
    conv_new = u_pad[:, -(CONV_WIDTH - 1):]

    qd = _rope(_rmsnorm(z_dq.reshape(B, T, H_DIFF, 2, DK_DIFF), lp["diff_qn"]), pos)
    kd = _rope(_rmsnorm(z_dk.reshape(B, T, H_DIFF, 2, DK_DIFF), lp["diff_kn"]), pos)
    vd = z_dv.reshape(B, T, H_DIFF, DV_DIFF)
    f32 = jnp.float32
    lam = (jnp.exp(jnp.sum(lp["lam_q1"].astype(f32) * lp["lam_k1"].astype(f32)))
           - jnp.exp(jnp.sum(lp["lam_q2"].astype(f32) * lp["lam_k2"].astype(f32))) + lam_init)

    qf = _rmsnorm(z_fq.reshape(B, T, H_FOX, D_FOX), lp["fox_qn"])
    kf = _rmsnorm(z_fk.reshape(B, T, H_FOX, D_FOX), lp["fox_kn"])
    vf = z_fv.reshape(B, T, H_FOX, D_FOX)
    lf = jax.nn.log_sigmoid(z_ff.astype(f32) + lp["fox_fb"].astype(f32))

    if past is None:
        F_new = _forget_prefix(lf)
        diff_segs = [(kd, vd, pos)]
        fox_segs = [(kf, vf, F_new, pos)]
    else:
        kd_p, vd_p, kf_p, vf_p, lf_p = past
        P = kd_p.shape[1]
        pos_p = jnp.arange(P, dtype=jnp.int32)
        F_all = _forget_prefix(jnp.concatenate([lf_p.astype(f32), lf], axis=1))
        F_new = F_all[:, P:]
        diff_segs = [(kd_p.reshape(B, P, H_DIFF, 2, DK_DIFF), vd_p, pos_p), (kd, vd, pos)]
        fox_segs = [(kf_p, vf_p, F_all[:, :P], pos_p), (kf, vf, F_new, pos)]

    def diff_block(s0, blk):
        qb = lax.dynamic_slice_in_dim(qd, s0, blk, axis=1)
        qp = lax.dynamic_slice_in_dim(pos, s0, blk)
        return _diff_attend(qb, qp, diff_segs, lam)

    def fox_block(s0, blk):
        qb = lax.dynamic_slice_in_dim(qf, s0, blk, axis=1)
        fb = lax.dynamic_slice_in_dim(F_new, s0, blk, axis=1)
        qp = lax.dynamic_slice_in_dim(pos, s0, blk)
        return _fox_attend(qb, fb, qp, fox_segs)

    yd = _rmsnorm(_sweep(diff_block, T), lp["diff_subln"]) * (1.0 - lam_init)
    yf = _sweep(fox_block, T)
    mix = jnp.concatenate([yc, yd.reshape(B, T, -1).astype(x.dtype),
                           yf.reshape(B, T, -1).astype(x.dtype)], axis=-1)
    x = x + g1 * (mix @ lp["w_out"])

    h2 = _rmsnorm(x, lp["norm_ffn"]) * (1 + sc2) + sh2
    a, b = jnp.split(h2 @ lp["w_ffn_in"], 2, axis=-1)
    x = x + g2 * ((jax.nn.silu(a) * b) @ lp["w_ffn_out"])

    state = (kd.reshape(B, T, H_DIFF, 2 * DK_DIFF), vd, kf, vf, lf, conv_new)
    return x, state


def setup_inputs(seed: int = 0) -> dict:
    key = jax.random.key(seed)
    ks = jax.random.split(key, 40)
    f32 = jnp.float32
    n_pages = PAST_LEN // PAGE_SIZE
    n_used = DEC_BATCH * n_pages
    n_pool = n_used + n_used // 4

    def nrm(k, shape, s):
        return jax.random.normal(k, shape, f32) * s

    def gain(k, shape):
        return 1.0 + nrm(k, shape, 0.02)

    page_table = jax.random.permutation(ks[8], n_pool)[:n_used].reshape(DEC_BATCH, n_pages).astype(jnp.int32)
    return {
        "x_prompt": nrm(ks[0], (BATCH, SEQ, D_MODEL), 1.0),
        "x_sample": nrm(ks[1], (DEC_BATCH, DEC_SEQ, D_MODEL), 1.0),
        "cache_diff_k": nrm(ks[2], (DEPTH, n_pool, PAGE_SIZE, H_DIFF, 2 * DK_DIFF), 1.0),
        "cache_diff_v": nrm(ks[3], (DEPTH, n_pool, PAGE_SIZE, H_DIFF, DV_DIFF), 1.0),
        "cache_fox_k": nrm(ks[4], (DEPTH, n_pool, PAGE_SIZE, H_FOX, D_FOX), 1.0),
        "cache_fox_v": nrm(ks[5], (DEPTH, n_pool, PAGE_SIZE, H_FOX, D_FOX), 1.0),
        "cache_fox_lf": jax.nn.log_sigmoid(2.0 + nrm(ks[6], (DEPTH, n_pool, PAGE_SIZE, H_FOX), 1.0)),
        "state_conv": nrm(ks[7], (DEPTH, DEC_BATCH, CONV_WIDTH - 1, C_CONV), 1.0),
        "page_table": page_table,
        "c_prompt": nrm(ks[9], (BATCH, D_MODEL), 1.0),
        "c_sample": nrm(ks[10], (DEC_BATCH, D_MODEL), 1.0),
        "norm_mix": gain(ks[11], (DEPTH, D_MODEL)),
        "norm_ffn": gain(ks[12], (DEPTH, D_MODEL)),
        "w_ada": nrm(ks[13], (DEPTH, D_MODEL, 6 * D_MODEL), 0.5 * D_MODEL ** -0.5),
        "b_ada": nrm(ks[14], (DEPTH, 6 * D_MODEL), 0.02),
        "w_in": nrm(ks[15], (DEPTH, D_MODEL, N_IN), D_MODEL ** -0.5),
        "conv_w": nrm(ks[16], (DEPTH, CONV_WIDTH, C_CONV), CONV_WIDTH ** -0.5),
        "conv_b": nrm(ks[17], (DEPTH, C_CONV), 0.02),
        "conv_ln_g": gain(ks[18], (DEPTH, C_CONV)),
        "conv_ln_b": nrm(ks[19], (DEPTH, C_CONV), 0.02),
        "diff_qn": gain(ks[20], (DEPTH, DK_DIFF)),
        "diff_kn": gain(ks[21], (DEPTH, DK_DIFF)),
        "lam_q1": nrm(ks[22], (DEPTH, DK_DIFF), 0.1),
        "lam_k1": nrm(ks[23], (DEPTH, DK_DIFF), 0.1),
        "lam_q2": nrm(ks[24], (DEPTH, DK_DIFF), 0.1),
        "lam_k2": nrm(ks[25], (DEPTH, DK_DIFF), 0.1),
        "diff_subln": gain(ks[26], (DEPTH, DV_DIFF)),
        "fox_qn": gain(ks[27], (DEPTH, D_FOX)),
        "fox_kn": gain(ks[28], (DEPTH, D_FOX)),
        "fox_fb": 2.0 + nrm(ks[29], (DEPTH, H_FOX), 0.1),
        "w_out": nrm(ks[30], (DEPTH, D_MIX, D_MODEL), D_MIX ** -0.5),
        "w_ffn_in": nrm(ks[31], (DEPTH, D_MODEL, 2 * D_FF), D_MODEL ** -0.5),
        "w_ffn_out": nrm(ks[32], (DEPTH, D_FF, D_MODEL), D_FF ** -0.5),
    }


def reference(x_prompt, x_sample, cache_diff_k, cache_diff_v, cache_fox_k, cache_fox_v,
              cache_fox_lf, state_conv, page_table, c_prompt, c_sample,
              norm_mix, norm_ffn, w_ada, b_ada, w_in, conv_w, conv_b, conv_ln_g, conv_ln_b,
              diff_qn, diff_kn, lam_q1, lam_k1, lam_q2, lam_k2, diff_subln,
              fox_qn, fox_kn, fox_fb, w_out, w_ffn_in, w_ffn_out):
    past_len = page_table.shape[1] * cache_diff_k.shape[2]
    pos_prompt = jnp.arange(x_prompt.shape[1], dtype=jnp.int32)
    pos_sample = past_len + jnp.arange(x_sample.shape[1], dtype=jnp.int32)
    conv0 = jnp.zeros((x_prompt.shape[0], CONV_WIDTH - 1, C_CONV), x_prompt.dtype)
    xp, xs = x_prompt, x_sample
    st_p, st_s = [], []
    for l in range(DEPTH):
        lp = {"norm_mix": norm_mix[l], "norm_ffn": norm_ffn[l], "w_ada": w_ada[l], "b_ada": b_ada[l],
              "w_in": w_in[l], "conv_w": conv_w[l], "conv_b": conv_b[l],
              "conv_ln_g": conv_ln_g[l], "conv_ln_b": conv_ln_b[l],
              "diff_qn": diff_qn[l], "diff_kn": diff_kn[l], "lam_q1": lam_q1[l], "lam_k1": lam_k1[l],
              "lam_q2": lam_q2[l], "lam_k2": lam_k2[l], "diff_subln": diff_subln[l],
              "fox_qn": fox_qn[l], "fox_kn": fox_kn[l], "fox_fb": fox_fb[l],
              "w_out": w_out[l], "w_ffn_in": w_ffn_in[l], "w_ffn_out": w_ffn_out[l]}
        lam_init = 0.8 - 0.6 * math.exp(-0.3 * l)
        xp, sp = _layer(xp, c_prompt, pos_prompt, None, conv0, lam_init, lp)
        past = (_gather_pages(cache_diff_k[l], page_table), _gather_pages(cache_diff_v[l], page_table),
                _gather_pages(cache_fox_k[l], page_table), _gather_pages(cache_fox_v[l], page_table),
                _gather_pages(cache_fox_lf[l], page_table))
        xs, ss = _layer(xs, c_sample, pos_sample, past, state_conv[l], lam_init, lp)
        st_p.append(sp)
        st_s.append(ss)

    def stk(sts, i):
        return jnp.stack([s[i] for s in sts], axis=0)

    return (xp, xs,
            stk(st_p, 0), stk(st_p, 1), stk(st_p, 2), stk(st_p, 3), stk(st_p, 4), stk(st_p, 5),
            stk(st_s, 0), stk(st_s, 1), stk(st_s, 2), stk(st_s, 3), stk(st_s, 4), stk(st_s, 5))
```

```python
import functools
import math

import jax
import jax.numpy as jnp
from jax import lax
from jax.experimental import pallas as pl
from jax.experimental.pallas import tpu as pltpu

F32 = jnp.float32
BF16 = jnp.bfloat16

EPS = 1e-6
NEG_INF = -1e30
ROPE_THETA = 10000.0
C_CONV = 512
CONV_WIDTH = 31
CONV_HIST = CONV_WIDTH - 1
N_HEADS = 6
HEAD_DIM = 128
DK_DIFF = 64
ATT_W = N_HEADS * HEAD_DIM

LANES = 128
SUBLANES = 8
V7X_VMEM_BYTES = 64 * 1024 * 1024
VMEM_CAP = V7X_VMEM_BYTES - 8 * 1024 * 1024
HIST_PAD = 32
CONV_CHUNK = 64
BIAS_PARTS = 3

DEC_PAGES_PER_STEP = 8


def _dot(a, b):
    return jnp.dot(a, b, preferred_element_type=F32)


def _dot_nt(a, b):
    return lax.dot_general(a, b, (((1,), (1,)), ((), ())), preferred_element_type=F32)


def _cparams(n_grid, vmem_bytes):
    return pltpu.CompilerParams(dimension_semantics=("arbitrary",) * n_grid,
                                vmem_limit_bytes=int(min(max(vmem_bytes, 16 << 20), VMEM_CAP)))


def _tile(n, pref):
    t = min(n, pref)
    assert n % t == 0, (n, pref)
    return t


def _full_spec(shape):
    nd = len(shape)
    return pl.BlockSpec(shape, lambda *_: (0,) * nd)


def _layer_spec(w, layer):
    return pl.BlockSpec((None,) + tuple(w.shape[1:]), lambda *_: (layer, 0, 0))


def _mod_spec(mod, tiles_per_group):
    _, r, d = mod.shape
    return pl.BlockSpec((None, r, d), lambda i, *_: (i // tiles_per_group, 0, 0))


def _split3(x):
    hi = x.astype(BF16).astype(F32)
    r1 = x - hi
    mid = r1.astype(BF16).astype(F32)
    lo = (r1 - mid).astype(BF16).astype(F32)
    return hi, mid, lo


def _ada_kernel(c_ref, w_ref, b_ref, o_ref):
    c = c_ref[...]
    a = c * jax.nn.sigmoid(c)
    a_hi = a.astype(BF16)
    a_lo = (a - a_hi.astype(F32)).astype(BF16)
    w = w_ref[...]
    w_hi = w.astype(BF16)
    w_lo = (w - w_hi.astype(F32)).astype(BF16)
    o_ref[...] = _dot(a_hi, w_hi) + _dot(a_hi, w_lo) + _dot(a_lo, w_hi) + b_ref[...]


def _ada(c_all, w_ada, b_ada):
    depth, d, n = w_ada.shape
    rows = c_all.shape[0]
    tn = _tile(n, 512)
    return pl.pallas_call(
        _ada_kernel, name="ada",
        grid=(depth, n // tn),
        in_specs=[pl.BlockSpec((rows, d), lambda l, j: (0, 0)),
                  pl.BlockSpec((None, d, tn), lambda l, j: (l, 0, j)),
                  pl.BlockSpec((None, 1, tn), lambda l, j: (l, 0, j))],
        out_specs=pl.BlockSpec((None, rows, tn), lambda l, j: (l, 0, j)),
        out_shape=jax.ShapeDtypeStruct((depth, rows, n), F32),
        compiler_params=_cparams(2, 6 * d * tn * 4),
    )(c_all, w_ada, b_ada.reshape(depth, 1, n))


def _norm_mod_value(x, g, sc, sh):
    y = x * lax.rsqrt(jnp.mean(x * x, axis=-1, keepdims=True) + EPS) * g
    return y * (1.0 + sc) + sh


def _norm_mod_kernel(x_ref, g_ref, sc_ref, sh_ref, h_ref):
    h_ref[...] = _norm_mod_value(x_ref[...], g_ref[...], sc_ref[...], sh_ref[...]).astype(h_ref.dtype)


def _norm_mod(x, g, sc, sh, tm, tiles_per_group):
    m, d = x.shape
    return pl.pallas_call(
        _norm_mod_kernel, name="norm_mod",
        grid=(m // tm,),
        in_specs=[pl.BlockSpec((tm, d), lambda i: (i, 0)), _full_spec((1, d)),
                  _mod_spec(sc, tiles_per_group), _mod_spec(sh, tiles_per_group)],
        out_specs=pl.BlockSpec((tm, d), lambda i: (i, 0)),
        out_shape=jax.ShapeDtypeStruct((m, d), BF16),
        compiler_params=_cparams(1, 8 * tm * d * 4),
    )(x, g, sc, sh)


def _conv_rows(ext_ref, cw_ref, cb_ref, lg_ref, lb_ref, r0, n):
    acc = jnp.broadcast_to(cb_ref[...], (n, C_CONV))
    base = HIST_PAD - CONV_HIST + r0
    for j in range(CONV_WIDTH):
        acc = acc + cw_ref[j:j + 1, :] * ext_ref[base + j:base + j + n, :]
    mu = jnp.mean(acc, axis=-1, keepdims=True)
    xc = acc - mu
    var = jnp.mean(xc * xc, axis=-1, keepdims=True)
    y = xc * lax.rsqrt(var + EPS) * lg_ref[...] + lb_ref[...]
    return y * jax.nn.sigmoid(y)


def _proj_conv_kernel(*refs, nseq, rows, carry):
    if carry:
        h_ref, wv_ref, wg_ref, cw_ref, cb_ref, lg_ref, lb_ref, yc_ref, cn_ref, ext_ref = refs
        st_ref = None
    else:
        h_ref, wv_ref, wg_ref, cw_ref, cb_ref, lg_ref, lb_ref, st_ref, yc_ref, cn_ref, ext_ref = refs
    hb = h_ref[...]
    u = _dot(hb, wv_ref[...]) * jax.nn.sigmoid(_dot(hb, wg_ref[...]))
    chunk = min(rows, CONV_CHUNK)
    for s in range(nseq):
        if carry:
            @pl.when(pl.program_id(1) == 0)
            def _():
                ext_ref[0:HIST_PAD, :] = jnp.zeros((HIST_PAD, C_CONV), F32)
        else:
            ext_ref[0:HIST_PAD, :] = st_ref[s]
        ext_ref[HIST_PAD:HIST_PAD + rows, :] = u[s * rows:(s + 1) * rows, :]
        for r0 in range(0, rows, chunk):
            y = _conv_rows(ext_ref, cw_ref, cb_ref, lg_ref, lb_ref, r0, chunk)
            yc_ref[s * rows + r0:s * rows + r0 + chunk, :] = y.astype(yc_ref.dtype)
        new_hist = ext_ref[rows + HIST_PAD - CONV_HIST:rows + HIST_PAD, :]
        if carry:
            tail = ext_ref[rows:rows + HIST_PAD, :]
            ext_ref[0:HIST_PAD, :] = tail

            @pl.when(pl.program_id(1) == pl.num_programs(1) - 1)
            def _():
                cn_ref[0] = new_hist
        else:
            cn_ref[s] = new_hist


def _proj_conv(h, w_cv, w_cg, layer, conv_w, conv_b, ln_g, ln_b, state_pad, nb, t, tm, y_dtype):
    m, d = h.shape
    carry = state_pad is None
    small = [_layer_spec(w_cv, layer), _layer_spec(w_cg, layer), _full_spec((CONV_WIDTH, C_CONV)),
             _full_spec((1, C_CONV)), _full_spec((1, C_CONV)), _full_spec((1, C_CONV))]
    if carry:
        assert tm >= HIST_PAD
        nt = t // tm
        grid = (nb, nt)
        in_specs = [pl.BlockSpec((tm, d), lambda b, i: (b * nt + i, 0))] + small
        out_specs = [pl.BlockSpec((tm, C_CONV), lambda b, i: (b * nt + i, 0)),
                     pl.BlockSpec((1, CONV_HIST, C_CONV), lambda b, i: (b, 0, 0))]
        kern = functools.partial(_proj_conv_kernel, nseq=1, rows=tm, carry=True)
        args = (h, w_cv, w_cg, conv_w, conv_b, ln_g, ln_b)
        rows = tm
    else:
        grid = (1, 1)
        in_specs = [_full_spec((m, d))] + small + [_full_spec((nb, HIST_PAD, C_CONV))]
        out_specs = [_full_spec((m, C_CONV)), _full_spec((nb, CONV_HIST, C_CONV))]
        kern = functools.partial(_proj_conv_kernel, nseq=nb, rows=t, carry=False)
        args = (h, w_cv, w_cg, conv_w, conv_b, ln_g, ln_b, state_pad)
        rows = t
    return pl.pallas_call(
        kern, name="proj_conv", grid=grid, in_specs=in_specs, out_specs=out_specs,
        out_shape=[jax.ShapeDtypeStruct((m, C_CONV), y_dtype),
                   jax.ShapeDtypeStruct((nb, CONV_HIST, C_CONV), F32)],
        scratch_shapes=[pltpu.VMEM((HIST_PAD + rows, C_CONV), F32)],
        compiler_params=_cparams(2, 24 << 20),
    )(*args)


def _store_heads(ref, hh, x, nseq, rows):
    for s in range(nseq):
        ref[s, hh] = x[s * rows:(s + 1) * rows, :].astype(ref.dtype)


def _halfnorm_rope(x, g, cos, sin, lane):
    sq = x * x
    lo = lane < DK_DIFF
    s_lo = jnp.sum(jnp.where(lo, sq, 0.0), axis=-1, keepdims=True)
    s_hi = jnp.sum(jnp.where(lo, 0.0, sq), axis=-1, keepdims=True)
    ms = jnp.where(lo, s_lo, s_hi) * (1.0 / DK_DIFF)
    y = x * lax.rsqrt(ms + EPS) * g
    half = DK_DIFF // 2
    rot = jnp.where((lane & (DK_DIFF - 1)) < half,
                    pltpu.roll(y, LANES - half, 1), pltpu.roll(y, half, 1))
    return y * cos + rot * sin


def _proj_diff_kernel(*refs, nseq, rows, n_alias, attn_ops):
    h_ref, wq_ref, wk_ref, wv_ref, gq_ref, gk_ref, cos_ref, sin_ref = refs[:8]
    outs = refs[8 + n_alias:]
    q_ref, ks_ref, vs_ref = outs[:3]
    hb = h_ref[...]
    tm = hb.shape[0]
    cos = cos_ref[...]
    sin = sin_ref[...]
    lane = lax.broadcasted_iota(jnp.int32, (tm, LANES), 1)
    zv = _dot(hb, wv_ref[...])
    zq = _dot(hb, wq_ref[...])
    zk = _dot(hb, wk_ref[...])
    scale = DK_DIFF ** -0.5
    for hh in range(N_HEADS):
        sl = slice(hh * HEAD_DIM, (hh + 1) * HEAD_DIM)
        q = _halfnorm_rope(zq[:, sl], gq_ref[...], cos, sin, lane) * scale
        q_ref[:, sl] = q.astype(q_ref.dtype)
        k = _halfnorm_rope(zk[:, sl], gk_ref[...], cos, sin, lane)
        v = zv[:, sl]
        _store_heads(ks_ref, hh, k, nseq, rows)
        _store_heads(vs_ref, hh, v, nseq, rows)
        if attn_ops:
            kb_ref, vt_ref = outs[3:5]
            kb_ref[0, hh] = k.astype(BF16)
            vt_ref[0, hh] = v.T.astype(BF16)


def _state_out(stack_shape, layer, nseq, rows):
    spec = pl.BlockSpec((None, nseq, N_HEADS, rows, HEAD_DIM), lambda b, i: (layer, b, 0, i, 0))
    return spec, jax.ShapeDtypeStruct(stack_shape, F32)


def _attn_operand_out(nb, t, tm, kd):
    specs = [pl.BlockSpec((1, N_HEADS, tm, kd), lambda b, i: (b, 0, i, 0)),
             pl.BlockSpec((1, N_HEADS, HEAD_DIM, tm), lambda b, i: (b, 0, 0, i))]
    shapes = [jax.ShapeDtypeStruct((nb, N_HEADS, t, kd), BF16),
              jax.ShapeDtypeStruct((nb, N_HEADS, HEAD_DIM, t), BF16)]
    return specs, shapes


def _alias_args(prev, n_in, first_out):
    if prev is None:
        return [], [], {}
    specs = [pl.BlockSpec(memory_space=pl.ANY)] * len(prev)
    return list(prev), specs, {n_in + k: first_out + k for k in range(len(prev))}


def _proj_diff(h, wq, wk, wv, layer, depth, gq, gk, cos, sin, prev, nb, t, tm, is_prompt):
    m, d = h.shape
    if is_prompt:
        nt, grid, nseq, rows = t // tm, (nb, t // tm), 1, tm
    else:
        nt, grid, nseq, rows = 1, (1, 1), nb, t
    row = lambda b, i: (b * nt + i, 0)
    att = pl.BlockSpec((tm, ATT_W), row)
    rope = pl.BlockSpec((tm, LANES), lambda b, i: (i, 0))
    sspec, sshape = _state_out((depth, nb, N_HEADS, t, HEAD_DIM), layer, nseq, rows)
    out_specs = [att, sspec, sspec]
    out_shape = [jax.ShapeDtypeStruct((m, ATT_W), BF16 if is_prompt else F32), sshape, sshape]
    if is_prompt:
        aspecs, ashapes = _attn_operand_out(nb, t, tm, HEAD_DIM)
        out_specs += aspecs
        out_shape += ashapes
    in_specs = [pl.BlockSpec((tm, d), row), _layer_spec(wq, layer), _layer_spec(wk, layer),
                _layer_spec(wv, layer), _full_spec((1, LANES)), _full_spec((1, LANES)), rope, rope]
    alias_in, alias_specs, aliases = _alias_args(prev, len(in_specs), 1)
    return pl.pallas_call(
        functools.partial(_proj_diff_kernel, nseq=nseq, rows=rows, n_alias=len(alias_in), attn_ops=is_prompt),
        name="proj_diff", grid=grid, in_specs=in_specs + alias_specs, out_specs=out_specs, out_shape=out_shape,
        input_output_aliases=aliases,
        compiler_params=_cparams(2, 44 << 20),
    )(h, wq, wk, wv, gq, gk, cos, sin, *alias_in)


def _scan_lanes(x, seg):
    lane = lax.broadcasted_iota(jnp.int32, x.shape, 1)
    pos = lane & (seg - 1)
    s = 1
    while s < seg:
        x = x + jnp.where(pos >= s, pltpu.roll(x, s, 1), 0.0)
        s *= 2
    return x


def _proj_fox_kernel(*refs, nseq, rows, seg, n_alias, attn_ops):
    h_ref, wq_ref, wk_ref, wv_ref, wf_ref, fb_ref, gq_ref, gk_ref = refs[:8]
    outs = refs[8 + n_alias:-1]
    carry_ref = refs[-1]
    q_ref, ks_ref, vs_ref, lf_ref, bias_ref = outs[:5]
    hb = h_ref[...]
    tm = hb.shape[0]
    z = _dot(hb, wf_ref[...]) + fb_ref[...]
    lf = jnp.minimum(z, 0.0) - jnp.log1p(jnp.exp(-jnp.abs(z)))
    if tm < LANES:
        lf = jnp.concatenate([lf, jnp.zeros((LANES - tm, LANES), F32)], axis=0)
    tb = lf.shape[0]
    lft = lf.T[:SUBLANES, :]
    lf_ref[...] = lft
    blocks = []
    if seg >= LANES:
        @pl.when(pl.program_id(1) == 0)
        def _():
            carry_ref[...] = jnp.zeros_like(carry_ref)
        c = carry_ref[...]
        for kb in range(tb // LANES):
            blk = _scan_lanes(lft[:, kb * LANES:(kb + 1) * LANES], LANES) + c
            blocks.append(-blk)
            c = jnp.broadcast_to(blk[:, LANES - 1:LANES], blk.shape)
        carry_ref[...] = c
    else:
        for kb in range(tb // LANES):
            blocks.append(-_scan_lanes(lft[:, kb * LANES:(kb + 1) * LANES], seg))
    bias = jnp.concatenate(blocks, axis=1) if len(blocks) > 1 else blocks[0]
    bias_ref[...] = bias
    if attn_ops:
        bias_t = jnp.concatenate([bias, jnp.zeros((LANES - SUBLANES, tb), F32)], axis=0).T
        lane = lax.broadcasted_iota(jnp.int32, (tm, LANES), 1)

    zv = _dot(hb, wv_ref[...])
    zq = _dot(hb, wq_ref[...])
    zk = _dot(hb, wk_ref[...])
    scale = HEAD_DIM ** -0.5
    for hh in range(N_HEADS):
        sl = slice(hh * HEAD_DIM, (hh + 1) * HEAD_DIM)
        x = zq[:, sl]
        q = x * lax.rsqrt(jnp.mean(x * x, axis=-1, keepdims=True) + EPS) * gq_ref[...] * scale
        q_ref[:, sl] = q.astype(q_ref.dtype)
        x = zk[:, sl]
        k = x * lax.rsqrt(jnp.mean(x * x, axis=-1, keepdims=True) + EPS) * gk_ref[...]
        v = zv[:, sl]
        _store_heads(ks_ref, hh, k, nseq, rows)
        _store_heads(vs_ref, hh, v, nseq, rows)
        if attn_ops:
            kb_ref, vt_ref = outs[5:7]
            hi, mid, lo = _split3(jnp.broadcast_to(bias_t[:, hh:hh + 1], (tm, LANES)))
            aug = jnp.where(lane == 0, hi, jnp.where(lane == 1, mid, jnp.where(lane == 2, lo, 0.0)))
            kb_ref[0, hh, :, 0:HEAD_DIM] = k.astype(BF16)
            kb_ref[0, hh, :, HEAD_DIM:2 * HEAD_DIM] = aug.astype(BF16)
            vt_ref[0, hh] = v.T.astype(BF16)


def _proj_fox(h, wq, wk, wv, wf, layer, depth, fb, gq, gk, prev, nb, t, tm, is_prompt):
    m, d = h.shape
    if is_prompt:
        nt, grid, nseq, rows, ng = t // tm, (nb, t // tm), 1, tm, nb
    else:
        nt, grid, nseq, rows, ng = 1, (1, 1), nb, t, 1
    tb = max(tm, LANES)
    row = lambda b, i: (b * nt + i, 0)
    att = pl.BlockSpec((tm, ATT_W), row)
    tspec = pl.BlockSpec((None, SUBLANES, tb), lambda b, i: (b, 0, i))
    tshape = jax.ShapeDtypeStruct((ng, SUBLANES, nt * tb), F32)
    sspec, sshape = _state_out((depth, nb, N_HEADS, t, HEAD_DIM), layer, nseq, rows)
    out_specs = [att, sspec, sspec, tspec, tspec]
    out_shape = [jax.ShapeDtypeStruct((m, ATT_W), BF16 if is_prompt else F32), sshape, sshape, tshape, tshape]
    if is_prompt:
        aspecs, ashapes = _attn_operand_out(nb, t, tm, 2 * HEAD_DIM)
        out_specs += aspecs
        out_shape += ashapes
    in_specs = [pl.BlockSpec((tm, d), row), _layer_spec(wq, layer), _layer_spec(wk, layer),
                _layer_spec(wv, layer), _layer_spec(wf, layer), _full_spec((1, LANES)),
                _full_spec((1, LANES)), _full_spec((1, LANES))]
    alias_in, alias_specs, aliases = _alias_args(prev, len(in_specs), 1)
    return pl.pallas_call(
        functools.partial(_proj_fox_kernel, nseq=nseq, rows=rows, seg=t, n_alias=len(alias_in),
                          attn_ops=is_prompt),
        name="proj_fox", grid=grid, in_specs=in_specs + alias_specs, out_specs=out_specs, out_shape=out_shape,
        input_output_aliases=aliases,
        scratch_shapes=[pltpu.VMEM((SUBLANES, LANES), F32)],
        compiler_params=_cparams(2, 44 << 20),
    )(h, wq, wk, wv, wf, fb, gq, gk, *alias_in)


def _lambda(lq1_ref, lk1_ref, lq2_ref, lk2_ref, lam_init):
    a = jnp.sum(lq1_ref[...] * lk1_ref[...], axis=-1, keepdims=True)
    b = jnp.sum(lq2_ref[...] * lk2_ref[...], axis=-1, keepdims=True)
    return jnp.exp(a) - jnp.exp(b) + lam_init


def _diff_combine(a1, l1, a2, l2, lam, g_sub, lam_init):
    y = a1 / l1 - lam * (a2 / l2)
    y = y * lax.rsqrt(jnp.mean(y * y, axis=-1, keepdims=True) + EPS) * g_sub
    return y * (1.0 - lam_init)


def _flash_t(i, tk, k_ref, vt_ref, qts, q_of_col, m_ref, l_ref, acc_ref):
    ng = len(qts)
    for g in range(ng):
        m_ref[g] = jnp.full(m_ref.shape[1:], NEG_INF, F32)
        l_ref[g] = jnp.zeros(l_ref.shape[1:], F32)
        acc_ref[g] = jnp.zeros(acc_ref.shape[1:], F32)

    def step(j, masked):
        start = pl.multiple_of(j * tk, tk)
        k = k_ref[pl.ds(start, tk), :]
        vt = vt_ref[:, pl.ds(start, tk)]
        for g in range(ng):
            s = _dot(k, qts[g])
            if masked:
                key = lax.broadcasted_iota(jnp.int32, s.shape, 0)
                s = jnp.where(key <= q_of_col[g], s, NEG_INF)
            m_old = m_ref[g]
            m_new = jnp.maximum(m_old, jnp.max(s, axis=0, keepdims=True))
            alpha = jnp.exp(m_old - m_new)
            p = jnp.exp(s - m_new)
            l_ref[g] = alpha * l_ref[g] + jnp.sum(p, axis=0, keepdims=True)
            acc_ref[g] = alpha * acc_ref[g] + _dot(vt, p.astype(BF16))
            m_ref[g] = m_new

    def body(j, c):
        step(j, False)
        return c

    lax.fori_loop(0, i, body, 0)
    step(i, True)


def _diff_attn_kernel(q_ref, k_ref, vt_ref, lq1_ref, lk1_ref, lq2_ref, lk2_ref, gs_ref, o_ref,
                      m_ref, l_ref, acc_ref, *, tq, lam_init):
    qt = q_ref[...].astype(F32).T
    sub = lax.broadcasted_iota(jnp.int32, qt.shape, 0)
    qts = [jnp.where(sub < DK_DIFF, qt, 0.0).astype(BF16), jnp.where(sub < DK_DIFF, 0.0, qt).astype(BF16)]
    col = lax.broadcasted_iota(jnp.int32, (1, tq), 1)
    _flash_t(pl.program_id(2), tq, k_ref, vt_ref, qts, [col, col], m_ref, l_ref, acc_ref)
    lam = _lambda(lq1_ref, lk1_ref, lq2_ref, lk2_ref, lam_init)
    yt = acc_ref[0] / l_ref[0] - lam * (acc_ref[1] / l_ref[1])
    y = yt.T
    y = y * lax.rsqrt(jnp.mean(y * y, axis=-1, keepdims=True) + EPS) * gs_ref[...]
    o_ref[...] = (y * (1.0 - lam_init)).astype(o_ref.dtype)


def _fox_attn_kernel(q_ref, k_ref, vt_ref, o_ref, m_ref, l_ref, acc_ref, *, tq):
    qt = q_ref[...].astype(F32).T
    sub = lax.broadcasted_iota(jnp.int32, qt.shape, 0)
    ones = jnp.where(sub < BIAS_PARTS, 1.0, 0.0)
    qa = jnp.concatenate([qt, ones], axis=0).astype(BF16)
    half = tq // 2
    col = lax.broadcasted_iota(jnp.int32, (1, half), 1)
    _flash_t(pl.program_id(2), tq, k_ref, vt_ref, [qa[:, :half], qa[:, half:]], [col, col + half],
             m_ref, l_ref, acc_ref)
    yt = jnp.concatenate([acc_ref[0] / l_ref[0], acc_ref[1] / l_ref[1]], axis=1)
    o_ref[...] = yt.T.astype(o_ref.dtype)


def _attn_call(kern, name, q, k, vt, extra, extra_specs, nb, t, tq, ncols):
    nt = t // tq
    kd = k.shape[-1]
    qspec = pl.BlockSpec((tq, HEAD_DIM), lambda b, h, i: (b * nt + i, h))
    return pl.pallas_call(
        kern, name=name,
        grid=(nb, N_HEADS, nt),
        in_specs=[qspec,
                  pl.BlockSpec((None, None, t, kd), lambda b, h, i: (b, h, 0, 0)),
                  pl.BlockSpec((None, None, HEAD_DIM, t), lambda b, h, i: (b, h, 0, 0))] + extra_specs,
        out_specs=qspec,
        out_shape=jax.ShapeDtypeStruct(q.shape, BF16),
        scratch_shapes=[pltpu.VMEM((2, 1, ncols), F32), pltpu.VMEM((2, 1, ncols), F32),
                        pltpu.VMEM((2, HEAD_DIM, ncols), F32)],
        compiler_params=_cparams(3, 40 << 20),
    )(q, k, vt, *extra)


def _diff_attn(q, k, vt, lq1, lk1, lq2, lk2, g_sub, nb, t, tq, lam_init):
    vec = pl.BlockSpec((1, DK_DIFF), lambda b, h, i: (0, 0))
    return _attn_call(functools.partial(_diff_attn_kernel, tq=tq, lam_init=lam_init), "diff_attn", q, k, vt,
                      [lq1, lk1, lq2, lk2, g_sub],
                      [vec, vec, vec, vec, pl.BlockSpec((1, HEAD_DIM), lambda b, h, i: (0, 0))],
                      nb, t, tq, tq)


def _fox_attn(q, k, vt, nb, t, tq):
    assert tq % (2 * LANES) == 0
    return _attn_call(functools.partial(_fox_attn_kernel, tq=tq), "fox_attn", q, k, vt, [], [], nb, t, tq,
                      tq // 2)


def _head_rows(q, width, n_groups):
    qt = jnp.concatenate([q] * n_groups, axis=0)
    lane = lax.broadcasted_iota(jnp.int32, qt.shape, 1)
    grp = jnp.right_shift(lax.broadcasted_iota(jnp.int32, qt.shape, 0), 3)
    lo = grp * width
    return jnp.where((lane >= lo) & (lane < lo + width), qt, 0.0).astype(BF16)


def _expand_heads(x, n):
    return jnp.concatenate([jnp.broadcast_to(x[h:h + 1, :], (SUBLANES, n)) for h in range(N_HEADS)], axis=0)


def _flat_heads(ref):
    return jnp.concatenate([ref[h].astype(BF16) for h in range(N_HEADS)], axis=1)


def _dec_update(s, v_tiles, m_ref, l_ref, acc_ref):
    m_old = m_ref[...]
    m_new = jnp.maximum(m_old, jnp.max(s, axis=-1, keepdims=True))
    alpha = jnp.exp(m_old - m_new)
    p = jnp.exp(s - m_new)
    l_ref[...] = alpha * l_ref[...] + jnp.sum(p, axis=-1, keepdims=True)
    pb = p.astype(BF16)
    pv = None
    for r, v in enumerate(v_tiles):
        d = _dot(pb[:, r * LANES:(r + 1) * LANES], v)
        pv = d if pv is None else pv + d
    acc_ref[...] = alpha * acc_ref[...] + pv
    m_ref[...] = m_new


def _new_tokens(qr, kn_ref, vn_ref, bias_rows):
    def padded(ref):
        x = jnp.concatenate([ref[h] for h in range(N_HEADS)], axis=1)
        return jnp.concatenate([x, jnp.zeros((LANES - SUBLANES, ATT_W), F32)], axis=0).astype(BF16)

    s = _dot_nt(qr, padded(kn_ref))
    if bias_rows is not None:
        s = s + bias_rows
    keep = (lax.broadcasted_iota(jnp.int32, s.shape, 1)
            <= (lax.broadcasted_iota(jnp.int32, s.shape, 0) & (SUBLANES - 1)))
    return jnp.where(keep, s, NEG_INF), padded(vn_ref)


def _dec_init(q_ref, qr_ref, m_ref, l_ref, acc_ref, width, n_groups):
    qr_ref[...] = _head_rows(q_ref[...], width, n_groups)
    m_ref[...] = jnp.full_like(m_ref, NEG_INF)
    l_ref[...] = jnp.zeros_like(l_ref)
    acc_ref[...] = jnp.zeros_like(acc_ref)


def _diff_dec_kernel(pt_ref, q_ref, kn_ref, vn_ref, lq1_ref, lk1_ref, lq2_ref, lk2_ref, gs_ref, *rest,
                     npg, lam_init):
    k_refs = rest[:npg]
    v_refs = rest[npg:2 * npg]
    o_ref, qr_ref, m_ref, l_ref, acc_ref = rest[2 * npg:]
    j = pl.program_id(1)

    @pl.when(j == 0)
    def _():
        _dec_init(q_ref, qr_ref, m_ref, l_ref, acc_ref, DK_DIFF, 2 * N_HEADS)

    qr = qr_ref[...]
    s = jnp.concatenate([_dot_nt(qr, _flat_heads(k)) for k in k_refs], axis=1)
    _dec_update(s, [_flat_heads(v) for v in v_refs], m_ref, l_ref, acc_ref)

    @pl.when(j == pl.num_programs(1) - 1)
    def _():
        sn, vn = _new_tokens(qr, kn_ref, vn_ref, None)
        _dec_update(sn, [vn], m_ref, l_ref, acc_ref)
        lam = _lambda(lq1_ref, lk1_ref, lq2_ref, lk2_ref, lam_init)
        acc = acc_ref[...]
        l = l_ref[...]
        for h in range(N_HEADS):
            r1 = slice(2 * h * SUBLANES, (2 * h + 1) * SUBLANES)
            r2 = slice((2 * h + 1) * SUBLANES, (2 * h + 2) * SUBLANES)
            cl = slice(h * HEAD_DIM, (h + 1) * HEAD_DIM)
            o_ref[:, cl] = _diff_combine(acc[r1, cl], l[r1], acc[r2, cl], l[r2], lam, gs_ref[...], lam_init)


def _fox_dec_kernel(pt_ref, q_ref, kn_ref, vn_ref, bn_ref, *rest, npg):
    k_refs = rest[:npg]
    v_refs = rest[npg:2 * npg]
    lf_refs = rest[2 * npg:3 * npg]
    o_ref, qr_ref, m_ref, l_ref, acc_ref, carry_ref = rest[3 * npg:]
    j = pl.program_id(1)

    @pl.when(j == 0)
    def _():
        _dec_init(q_ref, qr_ref, m_ref, l_ref, acc_ref, HEAD_DIM, N_HEADS)
        carry_ref[...] = jnp.zeros_like(carry_ref)

    c = carry_ref[...]
    biases = []
    for lf in lf_refs:
        blk = _scan_lanes(lf[...], LANES) + c
        biases.append(-blk)
        c = jnp.broadcast_to(blk[:, LANES - 1:LANES], blk.shape)
    carry_ref[...] = c

    qr = qr_ref[...]
    s = jnp.concatenate([_dot_nt(qr, _flat_heads(k)) for k in k_refs], axis=1)
    s = s + _expand_heads(jnp.concatenate(biases, axis=1), npg * LANES)
    _dec_update(s, [_flat_heads(v) for v in v_refs], m_ref, l_ref, acc_ref)

    @pl.when(j == pl.num_programs(1) - 1)
    def _():
        sn, vn = _new_tokens(qr, kn_ref, vn_ref, _expand_heads(bn_ref[...] - c, LANES))
        _dec_update(sn, [vn], m_ref, l_ref, acc_ref)
        acc = acc_ref[...]
        l = l_ref[...]
        for h in range(N_HEADS):
            rs = slice(h * SUBLANES, (h + 1) * SUBLANES)
            cl = slice(h * HEAD_DIM, (h + 1) * HEAD_DIM)
            o_ref[:, cl] = acc[rs, cl] / l[rs]


def _page_specs(layer, n_pages, npg, block):
    nz = (0,) * (len(block) - 2)

    def spec(r):
        return pl.BlockSpec(block, lambda b, j, pt, r=r: (layer, pt[b * n_pages + j * npg + r]) + nz)
    return [spec(r) for r in range(npg)]


def _dec_common(layer, tnew, rows):
    tok = pl.BlockSpec((tnew, ATT_W), lambda b, j, pt: (b, 0))
    new_kv = pl.BlockSpec((None, None, N_HEADS, tnew, HEAD_DIM), lambda b, j, pt: (layer, b, 0, 0, 0))
    scratch = [pltpu.VMEM((rows, ATT_W), BF16), pltpu.VMEM((rows, 1), F32), pltpu.VMEM((rows, 1), F32),
               pltpu.VMEM((rows, ATT_W), F32)]
    return tok, new_kv, scratch


def _diff_decode(pt, q, kn, vn, lq1, lk1, lq2, lk2, g_sub, cache_k, cache_v, layer, nb, n_pages, lam_init):
    tnew = q.shape[0] // nb
    assert tnew == SUBLANES
    page = cache_k.shape[3]
    npg = _tile(n_pages, DEC_PAGES_PER_STEP)
    tok, new_kv, scratch = _dec_common(layer, tnew, 2 * N_HEADS * SUBLANES)
    vec = pl.BlockSpec((1, DK_DIFF), lambda b, j, pt: (0, 0))
    pages = _page_specs(layer, n_pages, npg, (None, None, N_HEADS, page, HEAD_DIM))
    return pl.pallas_call(
        functools.partial(_diff_dec_kernel, npg=npg, lam_init=lam_init), name="diff_decode",
        grid_spec=pltpu.PrefetchScalarGridSpec(
            num_scalar_prefetch=1, grid=(nb, n_pages // npg),
            in_specs=[tok, new_kv, new_kv, vec, vec, vec, vec,
                      pl.BlockSpec((1, HEAD_DIM), lambda b, j, pt: (0, 0))] + pages + pages,
            out_specs=tok, scratch_shapes=scratch),
        out_shape=jax.ShapeDtypeStruct(q.shape, F32),
        compiler_params=_cparams(2, 40 << 20),
    )(pt, q, kn, vn, lq1, lk1, lq2, lk2, g_sub, *([cache_k] * npg), *([cache_v] * npg))


def _fox_decode(pt, q, kn, vn, bias_new, cache_k, cache_v, cache_lf, layer, nb, n_pages):
    tnew = q.shape[0] // nb
    assert tnew == SUBLANES
    page = cache_k.shape[3]
    assert page == LANES
    npg = _tile(n_pages, DEC_PAGES_PER_STEP)
    tok, new_kv, scratch = _dec_common(layer, tnew, N_HEADS * SUBLANES)
    pages = _page_specs(layer, n_pages, npg, (None, None, N_HEADS, page, HEAD_DIM))
    lf_pages = _page_specs(layer, n_pages, npg, (None, None, SUBLANES, page))
    return pl.pallas_call(
        functools.partial(_fox_dec_kernel, npg=npg), name="fox_decode",
        grid_spec=pltpu.PrefetchScalarGridSpec(
            num_scalar_prefetch=1, grid=(nb, n_pages // npg),
            in_specs=[tok, new_kv, new_kv,
                      pl.BlockSpec((None, SUBLANES, LANES), lambda b, j, pt: (b, 0, 0))]
            + pages + pages + lf_pages,
            out_specs=tok, scratch_shapes=scratch + [pltpu.VMEM((SUBLANES, LANES), F32)]),
        out_shape=jax.ShapeDtypeStruct(q.shape, F32),
        compiler_params=_cparams(2, 40 << 20),
    )(pt, q, kn, vn, bias_new, *([cache_k] * npg), *([cache_v] * npg), *([cache_lf] * npg))


def _out_proj_kernel(yc_ref, yd_ref, yf_ref, w_ref, x_ref, g1_ref, gn_ref, sc_ref, sh_ref, x1_ref, h2_ref):
    mix = jnp.concatenate([yc_ref[...].astype(BF16), yd_ref[...].astype(BF16), yf_ref[...].astype(BF16)], axis=1)
    x1 = x_ref[...] + g1_ref[...] * _dot(mix, w_ref[...])
    x1_ref[...] = x1
    h2_ref[...] = _norm_mod_value(x1, gn_ref[...], sc_ref[...], sh_ref[...]).astype(h2_ref.dtype)


def _out_proj(yc, yd, yf, w_out, layer, x, g1, gn, sc2, sh2, tm, tiles_per_group):
    m, d = x.shape
    row = lambda i: (i, 0)
    return pl.pallas_call(
        _out_proj_kernel, name="out_proj",
        grid=(m // tm,),
        in_specs=[pl.BlockSpec((tm, C_CONV), row), pl.BlockSpec((tm, ATT_W), row), pl.BlockSpec((tm, ATT_W), row),
                  _layer_spec(w_out, layer), pl.BlockSpec((tm, d), row), _mod_spec(g1, tiles_per_group),
                  _full_spec((1, d)), _mod_spec(sc2, tiles_per_group), _mod_spec(sh2, tiles_per_group)],
        out_specs=[pl.BlockSpec((tm, d), row), pl.BlockSpec((tm, d), row)],
        out_shape=[jax.ShapeDtypeStruct((m, d), F32), jax.ShapeDtypeStruct((m, d), BF16)],
        compiler_params=_cparams(1, 40 << 20),
    )(yc, yd, yf, w_out, x, g1, gn, sc2, sh2)


def _ffn_up_kernel(h_ref, wa_ref, wb_ref, g_ref):
    hb = h_ref[...]
    a = _dot(hb, wa_ref[...])
    b = _dot(hb, wb_ref[...])
    g_ref[...] = (a * jax.nn.sigmoid(a) * b).astype(g_ref.dtype)


def _ffn_up(h2, w_in, layer, tm, tf):
    m, d = h2.shape
    f = w_in.shape[2] // 2
    nf = f // tf
    return pl.pallas_call(
        _ffn_up_kernel, name="ffn_up",
        grid=(m // tm, nf),
        in_specs=[pl.BlockSpec((tm, d), lambda i, j: (i, 0)),
                  pl.BlockSpec((None, d, tf), lambda i, j: (layer, 0, j)),
                  pl.BlockSpec((None, d, tf), lambda i, j: (layer, 0, j + nf))],
        out_specs=pl.BlockSpec((tm, tf), lambda i, j: (i, j)),
        out_shape=jax.ShapeDtypeStruct((m, f), BF16),
        compiler_params=_cparams(2, 40 << 20),
    )(h2, w_in, w_in)


def _ffn_down_kernel(g_ref, w_ref, x_ref, g2_ref, *rest, next_norm):
    if next_norm:
        gn_ref, sc_ref, sh_ref, x2_ref, hn_ref, acc_ref = rest
    else:
        x2_ref, acc_ref = rest
    k = pl.program_id(1)

    @pl.when(k == 0)
    def _():
        acc_ref[...] = jnp.zeros_like(acc_ref)

    acc_ref[...] += _dot(g_ref[...], w_ref[...])

    @pl.when(k == pl.num_programs(1) - 1)
    def _():
        x2 = x_ref[...] + g2_ref[...] * acc_ref[...]
        x2_ref[...] = x2
        if next_norm:
            hn_ref[...] = _norm_mod_value(x2, gn_ref[...], sc_ref[...], sh_ref[...]).astype(hn_ref.dtype)


def _ffn_down(g, w_out, layer, x1, g2, nxt, tm, tk, tiles_per_group):
    m, f = g.shape
    d = w_out.shape[2]
    xspec = pl.BlockSpec((tm, d), lambda i, k: (i, 0))
    in_specs = [pl.BlockSpec((tm, tk), lambda i, k: (i, k)),
                pl.BlockSpec((None, tk, d), lambda i, k: (layer, k, 0)),
                xspec, _mod_spec(g2, tiles_per_group)]
    args = [g, w_out, x1, g2]
    out_specs, out_shape = [xspec], [jax.ShapeDtypeStruct((m, d), F32)]
    if nxt is not None:
        in_specs += [_full_spec((1, d)), _mod_spec(nxt[1], tiles_per_group), _mod_spec(nxt[2], tiles_per_group)]
        args += list(nxt)
        out_specs.append(xspec)
        out_shape.append(jax.ShapeDtypeStruct((m, d), BF16))
    out = pl.pallas_call(
        functools.partial(_ffn_down_kernel, next_norm=nxt is not None), name="ffn_down",
        grid=(m // tm, f // tk),
        in_specs=in_specs, out_specs=out_specs, out_shape=out_shape,
        scratch_shapes=[pltpu.VMEM((tm, d), F32)],
        compiler_params=_cparams(2, 48 << 20),
    )(*args)
    return (out[0], out[1]) if nxt is not None else (out[0], None)


def _rope_tables(pos):
    half = DK_DIFF // 2
    inv = ROPE_THETA ** (-jnp.arange(half, dtype=F32) / half)
    ang = pos.astype(F32)[:, None] * inv[None, :]
    cos = jnp.tile(jnp.cos(ang), (1, LANES // half))
    sin = jnp.sin(ang)
    sin = jnp.tile(jnp.concatenate([-sin, sin], axis=1), (1, LANES // DK_DIFF))
    return cos, sin


class _Group:
    def __init__(self, nb, t, is_prompt):
        self.nb, self.t, self.is_prompt = nb, t, is_prompt
        self.m = nb * t
        if is_prompt:
            self.tm = _tile(t, 256)
            self.tm_conv = _tile(t, 512)
            self.tm_ffn = _tile(t, 1024)
            self.tm_down = _tile(t, 512)
            self.tq = _tile(t, 512)
            self.down_tiles = t // self.tm_down
            self.tiles = t // self.tm
        else:
            self.tm = self.tm_conv = self.tm_ffn = self.tm_down = self.m
            self.down_tiles = self.tiles = 1


def _mods(ada, grp):
    d = ada.shape[1] // 6
    parts = jnp.split(ada, 6, axis=-1)
    if grp.is_prompt:
        return [p.reshape(grp.nb, 1, d) for p in parts]
    return [jnp.repeat(p, grp.t, axis=0).reshape(1, grp.m, d) for p in parts]


def _layer(x, h, grp, wts, lw, layer, depth, mods, nxt, rope, lam_init, states, ctx):
    sh1, sc1, g1, sh2, sc2, g2 = mods
    cos, sin = rope
    nb, t = grp.nb, grp.t
    prompt = grp.is_prompt
    prev_d = None if states is None else states[0:2]
    prev_f = None if states is None else states[2:4]

    yc, conv_new = _proj_conv(h, wts["w_cv"], wts["w_cg"], layer, lw["conv_w"], lw["conv_b"], lw["conv_ln_g"],
                              lw["conv_ln_b"], None if prompt else ctx["state_pad"],
                              nb, t, grp.tm_conv, BF16 if prompt else F32)
    dres = _proj_diff(h, wts["w_dq"], wts["w_dk"], wts["w_dv"], layer, depth, lw["diff_qn"], lw["diff_kn"],
                      cos, sin, prev_d, nb, t, grp.tm, prompt)
    fres = _proj_fox(h, wts["w_fq"], wts["w_fk"], wts["w_fv"], wts["w_ff"], layer, depth, lw["fox_fb"],
                     lw["fox_qn"], lw["fox_kn"], prev_f, nb, t, grp.tm, prompt)
    qd, kd, vd = dres[:3]
    qf, kf, vf, lft, fbias = fres[:5]
    if prompt:
        yd = _diff_attn(qd, dres[3], dres[4], lw["lam_q1"], lw["lam_k1"], lw["lam_q2"], lw["lam_k2"],
                        lw["diff_subln"], nb, t, grp.tq, lam_init)
        yf = _fox_attn(qf, fres[5], fres[6], nb, t, grp.tq)
    else:
        yd = _diff_decode(ctx["pt"], qd, kd, vd, lw["lam_q1"], lw["lam_k1"], lw["lam_q2"], lw["lam_k2"],
                          lw["diff_subln"], ctx["cache_dk"], ctx["cache_dv"], layer, nb, ctx["n_pages"], lam_init)
        bn = jnp.transpose(fbias[0, :, :nb * t].reshape(SUBLANES, nb, t), (1, 0, 2))
        bn = jnp.pad(bn, ((0, 0), (0, 0), (0, LANES - t)))
        yf = _fox_decode(ctx["pt"], qf, kf, vf, bn, ctx["cache_fk"], ctx["cache_fv"], ctx["cache_lf"],
                         layer, nb, ctx["n_pages"])
    x1, h2 = _out_proj(yc, yd, yf, wts["w_out"], layer, x, g1, lw["norm_ffn"], sc2, sh2, grp.tm, grp.tiles)
    g = _ffn_up(h2, wts["w_ffn_in"], layer, grp.tm_ffn, wts["tf"])
    x2, h_next = _ffn_down(g, wts["w_ffn_out"], layer, x1, g2, nxt, grp.tm_down, wts["tk"], grp.down_tiles)
    return x2, h_next, (kd, vd, kf, vf), lft, conv_new


def kernel(x_prompt, x_sample, cache_diff_k, cache_diff_v, cache_fox_k, cache_fox_v, cache_fox_lf, state_conv,
           page_table, c_prompt, c_sample, norm_mix, norm_ffn, w_ada, b_ada, w_in, conv_w, conv_b, conv_ln_g,
           conv_ln_b, diff_qn, diff_kn, lam_q1, lam_k1, lam_q2, lam_k2, diff_subln, fox_qn, fox_kn, fox_fb,
           w_out, w_ffn_in, w_ffn_out):
    depth = w_in.shape[0]
    bp, tp, d = x_prompt.shape
    bs, ts, _ = x_sample.shape
    page = cache_diff_k.shape[2]
    n_pages = page_table.shape[1]
    d_ff = w_ffn_out.shape[1]
    assert d_ff % LANES == 0
    gp = _Group(bp, tp, True)
    gs = _Group(bs, ts, False)

    c_all = jnp.concatenate([c_prompt, c_sample], axis=0)
    n_c = c_all.shape[0]
    c_all = jnp.pad(c_all, ((0, (-n_c) % (2 * SUBLANES)), (0, 0)))
    ada = _ada(c_all, w_ada, b_ada)
    mods_p = [_mods(ada[l, :bp], gp) for l in range(depth)]
    mods_s = [_mods(ada[l, bp:bp + bs], gs) for l in range(depth)]

    cuts = [0, C_CONV, 2 * C_CONV]
    for _ in range(6):
        cuts.append(cuts[-1] + ATT_W)
    names = ["w_cv", "w_cg", "w_dq", "w_dk", "w_dv", "w_fq", "w_fk", "w_fv"]
    wts = {n: w_in[:, :, cuts[i]:cuts[i + 1]].astype(BF16) for i, n in enumerate(names)}
    wts["w_ff"] = jnp.pad(w_in[:, :, cuts[-1]:], ((0, 0), (0, 0), (0, LANES - N_HEADS))).astype(BF16)
    wts["w_out"] = w_out.astype(BF16)
    wts["w_ffn_in"] = w_ffn_in.astype(BF16)
    wts["w_ffn_out"] = w_ffn_out.astype(BF16)
    wts["tf"] = _tile(d_ff, 512)
    wts["tk"] = d_ff // 4 if (d_ff // 4) % LANES == 0 else _tile(d_ff, 512)

    rope_p = _rope_tables(jnp.arange(tp, dtype=jnp.int32))
    cos_s, sin_s = _rope_tables(n_pages * page + jnp.arange(ts, dtype=jnp.int32))
    rope_s = (jnp.tile(cos_s, (bs, 1)), jnp.tile(sin_s, (bs, 1)))

    head_major = lambda c: jnp.transpose(c, (0, 1, 3, 2, 4))
    ctx = dict(pt=page_table.reshape(-1).astype(jnp.int32), n_pages=n_pages,
               cache_dk=head_major(cache_diff_k), cache_dv=head_major(cache_diff_v),
               cache_fk=head_major(cache_fox_k), cache_fv=head_major(cache_fox_v),
               cache_lf=jnp.pad(jnp.transpose(cache_fox_lf, (0, 1, 3, 2)),
                                ((0, 0), (0, 0), (0, SUBLANES - N_HEADS), (0, 0))))
    state_pad = jnp.pad(state_conv, ((0, 0), (0, 0), (HIST_PAD - CONV_HIST, 0), (0, 0)))

    xp = x_prompt.reshape(bp * tp, d)
    xs = x_sample.reshape(bs * ts, d)
    g0 = norm_mix[0][None, :]
    hp = _norm_mod(xp, g0, mods_p[0][1], mods_p[0][0], gp.tm_down, gp.down_tiles)
    hs = _norm_mod(xs, g0, mods_s[0][1], mods_s[0][0], gs.tm, 1)
    kv_p = kv_s = None
    lf_p, lf_s, conv_p, conv_s = [], [], [], []
    for l in range(depth):
        lam_init = 0.8 - 0.6 * math.exp(-0.3 * l)
        lw = dict(
            conv_w=conv_w[l], conv_b=conv_b[l][None, :], conv_ln_g=conv_ln_g[l][None, :],
            conv_ln_b=conv_ln_b[l][None, :],
            diff_qn=jnp.tile(diff_qn[l], LANES // DK_DIFF)[None, :],
            diff_kn=jnp.tile(diff_kn[l], LANES // DK_DIFF)[None, :],
            lam_q1=lam_q1[l][None, :], lam_k1=lam_k1[l][None, :], lam_q2=lam_q2[l][None, :],
            lam_k2=lam_k2[l][None, :], diff_subln=diff_subln[l][None, :],
            fox_qn=fox_qn[l][None, :], fox_kn=fox_kn[l][None, :],
            fox_fb=jnp.pad(fox_fb[l], (0, LANES - N_HEADS))[None, :],
            norm_ffn=norm_ffn[l][None, :])
        last = l == depth - 1
        nxt_p = None if last else (norm_mix[l + 1][None, :], mods_p[l + 1][1], mods_p[l + 1][0])
        nxt_s = None if last else (norm_mix[l + 1][None, :], mods_s[l + 1][1], mods_s[l + 1][0])
        xp, hp, kv_p, lft, cn = _layer(xp, hp, gp, wts, lw, l, depth, mods_p[l], nxt_p, rope_p, lam_init, kv_p,
                                       None)
        lf_p.append(lft)
        conv_p.append(cn)
        xs, hs, kv_s, lft, cn = _layer(xs, hs, gs, wts, lw, l, depth, mods_s[l], nxt_s, rope_s, lam_init, kv_s,
                                       dict(ctx, state_pad=state_pad[l]))
        lf_s.append(lft)
        conv_s.append(cn)

    kv = lambda a: jnp.transpose(a, (0, 1, 3, 2, 4))
    lf_p = jnp.transpose(jnp.stack(lf_p, axis=0)[:, :, :N_HEADS, :], (0, 1, 3, 2))
    lf_s = jnp.stack(lf_s, axis=0)[:, 0, :N_HEADS, :bs * ts]
    lf_s = jnp.transpose(lf_s.reshape(depth, N_HEADS, bs, ts), (0, 2, 3, 1))
    return (xp.reshape(bp, tp, d), xs.reshape(bs, ts, d),
            kv(kv_p[0]), kv(kv_p[1]), kv(kv_p[2]), kv(kv_p[3]), lf_p, jnp.stack(conv_p, axis=0),
            kv(kv_s[0]), kv(kv_s[1]), kv(kv_s[2]), kv(kv_s[3]), lf_s, jnp.stack(conv_s, axis=0))
```

```python
import functools
import math

import jax
import jax.numpy as jnp
from jax import lax
from jax.experimental import pallas as pl
from jax.experimental.pallas import tpu as pltpu

F32 = jnp.float32
BF16 = jnp.bfloat16

EPS = 1e-6
NEG_INF = -1e30
ROPE_THETA = 10000.0
C_CONV = 512
CONV_WIDTH = 31
CONV_HIST = CONV_WIDTH - 1
N_HEADS = 6
HEAD_DIM = 128
DK_DIFF = 64
ATT_W = N_HEADS * HEAD_DIM

LANES = 128
SUBLANES = 8
V7X_VMEM_BYTES = 64 * 1024 * 1024
VMEM_CAP = V7X_VMEM_BYTES - 8 * 1024 * 1024
HIST_PAD = 32
CONV_CHUNK = 64
BIAS_PARTS = 3
LOG2E = 1.4426950408889634
BF16_ROWS = 16

DEC_PAGES_PER_STEP = 16


def _dot(a, b):
    return jnp.dot(a, b, preferred_element_type=F32)


def _dot_nt(a, b):
    return lax.dot_general(a, b, (((1,), (1,)), ((), ())), preferred_element_type=F32)


def _cparams(n_grid, vmem_bytes):
    return pltpu.CompilerParams(dimension_semantics=("arbitrary",) * n_grid,
                                vmem_limit_bytes=int(min(max(vmem_bytes, 16 << 20), VMEM_CAP)))


def _tile(n, pref):
    t = min(n, pref)
    assert n % t == 0, (n, pref)
    return t


def _full_spec(shape):
    nd = len(shape)
    return pl.BlockSpec(shape, lambda *_: (0,) * nd)


def _layer_spec(w, layer):
    return pl.BlockSpec((None,) + tuple(w.shape[1:]), lambda *_: (layer, 0, 0))


def _mod_spec(mod, tiles_per_group):
    _, r, d = mod.shape
    return pl.BlockSpec((None, r, d), lambda i, *_: (i // tiles_per_group, 0, 0))


def _split3(x):
    hi = x.astype(BF16).astype(F32)
    r1 = x - hi
    mid = r1.astype(BF16).astype(F32)
    lo = (r1 - mid).astype(BF16).astype(F32)
    return hi, mid, lo


def _ada_kernel(c_ref, w_ref, b_ref, o_ref):
    c = c_ref[...]
    a = c * jax.nn.sigmoid(c)
    a_hi = a.astype(BF16)
    a_lo = (a - a_hi.astype(F32)).astype(BF16)
    w = w_ref[...]
    w_hi = w.astype(BF16)
    w_lo = (w - w_hi.astype(F32)).astype(BF16)
    o_ref[...] = _dot(a_hi, w_hi) + _dot(a_hi, w_lo) + _dot(a_lo, w_hi) + b_ref[...]


def _ada(c_all, w_ada, b_ada):
    depth, d, n = w_ada.shape
    rows = c_all.shape[0]
    tn = _tile(n, 512)
    return pl.pallas_call(
        _ada_kernel, name="ada",
        grid=(depth, n // tn),
        in_specs=[pl.BlockSpec((rows, d), lambda l, j: (0, 0)),
                  pl.BlockSpec((None, d, tn), lambda l, j: (l, 0, j)),
                  pl.BlockSpec((None, 1, tn), lambda l, j: (l, 0, j))],
        out_specs=pl.BlockSpec((None, rows, tn), lambda l, j: (l, 0, j)),
        out_shape=jax.ShapeDtypeStruct((depth, rows, n), F32),
        compiler_params=_cparams(2, 6 * d * tn * 4),
    )(c_all, w_ada, b_ada.reshape(depth, 1, n))


def _norm_mod_value(x, g, sc, sh):
    y = x * lax.rsqrt(jnp.mean(x * x, axis=-1, keepdims=True) + EPS) * g
    return y * (1.0 + sc) + sh


def _norm_mod_kernel(x_ref, g_ref, sc_ref, sh_ref, h_ref):
    h_ref[...] = _norm_mod_value(x_ref[...], g_ref[...], sc_ref[...], sh_ref[...]).astype(h_ref.dtype)


def _norm_mod(x, g, sc, sh, tm, tiles_per_group):
    m, d = x.shape
    return pl.pallas_call(
        _norm_mod_kernel, name="norm_mod",
        grid=(m // tm,),
        in_specs=[pl.BlockSpec((tm, d), lambda i: (i, 0)), _full_spec((1, d)),
                  _mod_spec(sc, tiles_per_group), _mod_spec(sh, tiles_per_group)],
        out_specs=pl.BlockSpec((tm, d), lambda i: (i, 0)),
        out_shape=jax.ShapeDtypeStruct((m, d), BF16),
        compiler_params=_cparams(1, 8 * tm * d * 4),
    )(x, g, sc, sh)


def _conv_rows(ext_ref, z_ref, cw_ref, cb_ref, lg_ref, lb_ref, r0, n):
    acc = jnp.broadcast_to(cb_ref[...], (n, C_CONV))
    off = HIST_PAD - CONV_HIST
    for r in range(SUBLANES):
        span = n if r == 0 else n + SUBLANES
        z = None
        for a in range((CONV_WIDTH + off) // SUBLANES + 1):
            j = SUBLANES * a + r - off
            if 0 <= j < CONV_WIDTH:
                term = cw_ref[j:j + 1, :] * ext_ref[r0 + SUBLANES * a:r0 + SUBLANES * a + span, :]
                z = term if z is None else z + term
        if r == 0:
            acc = acc + z
        else:
            z_ref[r] = z
            acc = acc + z_ref[r, r:r + n, :]
    mu = jnp.mean(acc, axis=-1, keepdims=True)
    xc = acc - mu
    var = jnp.mean(xc * xc, axis=-1, keepdims=True)
    y = xc * lax.rsqrt(var + EPS) * lg_ref[...] + lb_ref[...]
    return y * jax.nn.sigmoid(y)


def _proj_conv_kernel(*refs, nseq, rows, carry):
    if carry:
        h_ref, wv_ref, wg_ref, cw_ref, cb_ref, lg_ref, lb_ref, yc_ref, cn_ref, ext_ref, z_ref = refs
        st_ref = None
    else:
        h_ref, wv_ref, wg_ref, cw_ref, cb_ref, lg_ref, lb_ref, st_ref, yc_ref, cn_ref, ext_ref, z_ref = refs
    hb = h_ref[...]
    u = _dot(hb, wv_ref[...]) * jax.nn.sigmoid(_dot(hb, wg_ref[...]))
    chunk = min(rows, CONV_CHUNK)
    for s in range(nseq):
        if carry:
            @pl.when(pl.program_id(1) == 0)
            def _():
                ext_ref[0:HIST_PAD, :] = jnp.zeros((HIST_PAD, C_CONV), F32)
        else:
            ext_ref[0:HIST_PAD, :] = st_ref[s]
        ext_ref[HIST_PAD:HIST_PAD + rows, :] = u[s * rows:(s + 1) * rows, :]
        for r0 in range(0, rows, chunk):
            y = _conv_rows(ext_ref, z_ref, cw_ref, cb_ref, lg_ref, lb_ref, r0, chunk)
            yc_ref[s * rows + r0:s * rows + r0 + chunk, :] = y.astype(yc_ref.dtype)
        new_hist = ext_ref[rows + HIST_PAD - CONV_HIST:rows + HIST_PAD, :]
        if carry:
            tail = ext_ref[rows:rows + HIST_PAD, :]
            ext_ref[0:HIST_PAD, :] = tail

            @pl.when(pl.program_id(1) == pl.num_programs(1) - 1)
            def _():
                cn_ref[0] = new_hist
        else:
            cn_ref[s] = new_hist


def _proj_conv(h, w_cv, w_cg, layer, conv_w, conv_b, ln_g, ln_b, state_pad, nb, t, tm, y_dtype):
    m, d = h.shape
    carry = state_pad is None
    small = [_layer_spec(w_cv, layer), _layer_spec(w_cg, layer), _full_spec((CONV_WIDTH, C_CONV)),
             _full_spec((1, C_CONV)), _full_spec((1, C_CONV)), _full_spec((1, C_CONV))]
    if carry:
        assert tm >= HIST_PAD
        nt = t // tm
        grid = (nb, nt)
        in_specs = [pl.BlockSpec((tm, d), lambda b, i: (b * nt + i, 0))] + small
        out_specs = [pl.BlockSpec((tm, C_CONV), lambda b, i: (b * nt + i, 0)),
                     pl.BlockSpec((1, CONV_HIST, C_CONV), lambda b, i: (b, 0, 0))]
        kern = functools.partial(_proj_conv_kernel, nseq=1, rows=tm, carry=True)
        args = (h, w_cv, w_cg, conv_w, conv_b, ln_g, ln_b)
        rows = tm
    else:
        grid = (1, 1)
        in_specs = [_full_spec((m, d))] + small + [_full_spec((nb, HIST_PAD, C_CONV))]
        out_specs = [_full_spec((m, C_CONV)), _full_spec((nb, CONV_HIST, C_CONV))]
        kern = functools.partial(_proj_conv_kernel, nseq=nb, rows=t, carry=False)
        args = (h, w_cv, w_cg, conv_w, conv_b, ln_g, ln_b, state_pad)
        rows = t
    return pl.pallas_call(
        kern, name="proj_conv", grid=grid, in_specs=in_specs, out_specs=out_specs,
        out_shape=[jax.ShapeDtypeStruct((m, C_CONV), y_dtype),
                   jax.ShapeDtypeStruct((nb, CONV_HIST, C_CONV), F32)],
        scratch_shapes=[pltpu.VMEM((HIST_PAD + rows, C_CONV), F32),
                        pltpu.VMEM((SUBLANES, min(rows, CONV_CHUNK) + SUBLANES, C_CONV), F32)],
        compiler_params=_cparams(2, 24 << 20),
    )(*args)


def _store_heads(ref, hh, x, nseq, rows):
    for s in range(nseq):
        ref[s, hh] = x[s * rows:(s + 1) * rows, :].astype(ref.dtype)


def _halfnorm_rope(x, g, cos, sin, lane):
    sq = x * x
    lo = lane < DK_DIFF
    s_lo = jnp.sum(jnp.where(lo, sq, 0.0), axis=-1, keepdims=True)
    s_hi = jnp.sum(jnp.where(lo, 0.0, sq), axis=-1, keepdims=True)
    ms = jnp.where(lo, s_lo, s_hi) * (1.0 / DK_DIFF)
    y = x * lax.rsqrt(ms + EPS) * g
    half = DK_DIFF // 2
    rot = jnp.where((lane & (DK_DIFF - 1)) < half,
                    pltpu.roll(y, LANES - half, 1), pltpu.roll(y, half, 1))
    return y * cos + rot * sin


def _proj_diff_kernel(*refs, nseq, rows, n_alias, attn_ops):
    h_ref, wq_ref, wk_ref, wv_ref, gq_ref, gk_ref, cos_ref, sin_ref = refs[:8]
    outs = refs[8 + n_alias:]
    q_ref, ks_ref, vs_ref = outs[:3]
    hb = h_ref[...]
    tm = hb.shape[0]
    cos = cos_ref[...]
    sin = sin_ref[...]
    lane = lax.broadcasted_iota(jnp.int32, (tm, LANES), 1)
    zv = _dot(hb, wv_ref[...])
    zq = _dot(hb, wq_ref[...])
    zk = _dot(hb, wk_ref[...])
    scale = DK_DIFF ** -0.5 * (LOG2E if attn_ops else 1.0)
    for hh in range(N_HEADS):
        sl = slice(hh * HEAD_DIM, (hh + 1) * HEAD_DIM)
        q = _halfnorm_rope(zq[:, sl], gq_ref[...], cos, sin, lane) * scale
        q_ref[:, sl] = q.astype(q_ref.dtype)
        k = _halfnorm_rope(zk[:, sl], gk_ref[...], cos, sin, lane)
        v = zv[:, sl]
        _store_heads(ks_ref, hh, k, nseq, rows)
        _store_heads(vs_ref, hh, v, nseq, rows)
        if attn_ops:
            kb_ref, vt_ref = outs[3:5]
            kb_ref[0, hh] = k.astype(BF16)
            vt_ref[0, hh] = v.T.astype(BF16)


def _state_out(stack_shape, layer, nseq, rows):
    spec = pl.BlockSpec((None, nseq, N_HEADS, rows, HEAD_DIM), lambda b, i: (layer, b, 0, i, 0))
    return spec, jax.ShapeDtypeStruct(stack_shape, F32)


def _attn_operand_out(nb, t, tm, kd):
    specs = [pl.BlockSpec((1, N_HEADS, tm, kd), lambda b, i: (b, 0, i, 0)),
             pl.BlockSpec((1, N_HEADS, HEAD_DIM, tm), lambda b, i: (b, 0, 0, i))]
    shapes = [jax.ShapeDtypeStruct((nb, N_HEADS, t, kd), BF16),
              jax.ShapeDtypeStruct((nb, N_HEADS, HEAD_DIM, t), BF16)]
    return specs, shapes


def _alias_args(prev, n_in, first_out):
    if prev is None:
        return [], [], {}
    specs = [pl.BlockSpec(memory_space=pl.ANY)] * len(prev)
    return list(prev), specs, {n_in + k: first_out + k for k in range(len(prev))}


def _proj_diff(h, wq, wk, wv, layer, depth, gq, gk, cos, sin, prev, nb, t, tm, is_prompt):
    m, d = h.shape
    if is_prompt:
        nt, grid, nseq, rows = t // tm, (nb, t // tm), 1, tm
    else:
        nt, grid, nseq, rows = 1, (1, 1), nb, t
    row = lambda b, i: (b * nt + i, 0)
    att = pl.BlockSpec((tm, ATT_W), row)
    rope = pl.BlockSpec((tm, LANES), lambda b, i: (i, 0))
    sspec, sshape = _state_out((depth, nb, N_HEADS, t, HEAD_DIM), layer, nseq, rows)
    out_specs = [att, sspec, sspec]
    out_shape = [jax.ShapeDtypeStruct((m, ATT_W), BF16 if is_prompt else F32), sshape, sshape]
    if is_prompt:
        aspecs, ashapes = _attn_operand_out(nb, t, tm, HEAD_DIM)
        out_specs += aspecs
        out_shape += ashapes
    in_specs = [pl.BlockSpec((tm, d), row), _layer_spec(wq, layer), _layer_spec(wk, layer),
                _layer_spec(wv, layer), _full_spec((1, LANES)), _full_spec((1, LANES)), rope, rope]
    alias_in, alias_specs, aliases = _alias_args(prev, len(in_specs), 1)
    return pl.pallas_call(
        functools.partial(_proj_diff_kernel, nseq=nseq, rows=rows, n_alias=len(alias_in), attn_ops=is_prompt),
        name="proj_diff", grid=grid, in_specs=in_specs + alias_specs, out_specs=out_specs, out_shape=out_shape,
        input_output_aliases=aliases,
        compiler_params=_cparams(2, 44 << 20),
    )(h, wq, wk, wv, gq, gk, cos, sin, *alias_in)


def _scan_lanes(x, seg):
    lane = lax.broadcasted_iota(jnp.int32, x.shape, 1)
    pos = lane & (seg - 1)
    s = 1
    while s < seg:
        x = x + jnp.where(pos >= s, pltpu.roll(x, s, 1), 0.0)
        s *= 2
    return x


def _proj_fox_kernel(*refs, nseq, rows, seg, n_alias, attn_ops):
    h_ref, wq_ref, wk_ref, wv_ref, wf_ref, fb_ref, gq_ref, gk_ref = refs[:8]
    outs = refs[8 + n_alias:-1]
    carry_ref = refs[-1]
    q_ref, ks_ref, vs_ref, lf_ref, bias_ref = outs[:5]
    hb = h_ref[...]
    tm = hb.shape[0]
    z = _dot(hb, wf_ref[...]) + fb_ref[...]
    lf = jnp.minimum(z, 0.0) - jnp.log1p(jnp.exp(-jnp.abs(z)))
    if tm < LANES:
        lf = jnp.concatenate([lf, jnp.zeros((LANES - tm, LANES), F32)], axis=0)
    tb = lf.shape[0]
    lft = lf.T[:SUBLANES, :]
    lf_ref[...] = lft
    blocks = []
    if seg >= LANES:
        @pl.when(pl.program_id(1) == 0)
        def _():
            carry_ref[...] = jnp.zeros_like(carry_ref)
        c = carry_ref[...]
        for kb in range(tb // LANES):
            blk = _scan_lanes(lft[:, kb * LANES:(kb + 1) * LANES], LANES) + c
            blocks.append(-blk)
            c = jnp.broadcast_to(blk[:, LANES - 1:LANES], blk.shape)
        carry_ref[...] = c
    else:
        for kb in range(tb // LANES):
            blocks.append(-_scan_lanes(lft[:, kb * LANES:(kb + 1) * LANES], seg))
    bias = jnp.concatenate(blocks, axis=1) if len(blocks) > 1 else blocks[0]
    bias_ref[...] = bias
    if attn_ops:
        bias_t = jnp.concatenate([bias, jnp.zeros((LANES - SUBLANES, tb), F32)], axis=0).T
        lane = lax.broadcasted_iota(jnp.int32, (tm, LANES), 1)

    zv = _dot(hb, wv_ref[...])
    zq = _dot(hb, wq_ref[...])
    zk = _dot(hb, wk_ref[...])
    scale = HEAD_DIM ** -0.5 * (LOG2E if attn_ops else 1.0)
    for hh in range(N_HEADS):
        sl = slice(hh * HEAD_DIM, (hh + 1) * HEAD_DIM)
        x = zq[:, sl]
        q = x * lax.rsqrt(jnp.mean(x * x, axis=-1, keepdims=True) + EPS) * gq_ref[...] * scale
        q_ref[:, sl] = q.astype(q_ref.dtype)
        x = zk[:, sl]
        k = x * lax.rsqrt(jnp.mean(x * x, axis=-1, keepdims=True) + EPS) * gk_ref[...]
        v = zv[:, sl]
        _store_heads(ks_ref, hh, k, nseq, rows)
        _store_heads(vs_ref, hh, v, nseq, rows)
        if attn_ops:
            kb_ref, vt_ref = outs[5:7]
            hi, mid, lo = _split3(jnp.broadcast_to(bias_t[:, hh:hh + 1], (tm, LANES)) * LOG2E)
            aug = jnp.where(lane == 0, hi, jnp.where(lane == 1, mid, jnp.where(lane == 2, lo, 0.0)))
            kb_ref[0, hh, :, 0:HEAD_DIM] = k.astype(BF16)
            kb_ref[0, hh, :, HEAD_DIM:2 * HEAD_DIM] = aug.astype(BF16)
            vt_ref[0, hh] = v.T.astype(BF16)


def _proj_fox(h, wq, wk, wv, wf, layer, depth, fb, gq, gk, prev, nb, t, tm, is_prompt):
    m, d = h.shape
    if is_prompt:
        nt, grid, nseq, rows, ng = t // tm, (nb, t // tm), 1, tm, nb
    else:
        nt, grid, nseq, rows, ng = 1, (1, 1), nb, t, 1
    tb = max(tm, LANES)
    row = lambda b, i: (b * nt + i, 0)
    att = pl.BlockSpec((tm, ATT_W), row)
    tspec = pl.BlockSpec((None, SUBLANES, tb), lambda b, i: (b, 0, i))
    tshape = jax.ShapeDtypeStruct((ng, SUBLANES, nt * tb), F32)
    sspec, sshape = _state_out((depth, nb, N_HEADS, t, HEAD_DIM), layer, nseq, rows)
    out_specs = [att, sspec, sspec, tspec, tspec]
    out_shape = [jax.ShapeDtypeStruct((m, ATT_W), BF16 if is_prompt else F32), sshape, sshape, tshape, tshape]
    if is_prompt:
        aspecs, ashapes = _attn_operand_out(nb, t, tm, 2 * HEAD_DIM)
        out_specs += aspecs
        out_shape += ashapes
    in_specs = [pl.BlockSpec((tm, d), row), _layer_spec(wq, layer), _layer_spec(wk, layer),
                _layer_spec(wv, layer), _layer_spec(wf, layer), _full_spec((1, LANES)),
                _full_spec((1, LANES)), _full_spec((1, LANES))]
    alias_in, alias_specs, aliases = _alias_args(prev, len(in_specs), 1)
    return pl.pallas_call(
        functools.partial(_proj_fox_kernel, nseq=nseq, rows=rows, seg=t, n_alias=len(alias_in),
                          attn_ops=is_prompt),
        name="proj_fox", grid=grid, in_specs=in_specs + alias_specs, out_specs=out_specs, out_shape=out_shape,
        input_output_aliases=aliases,
        scratch_shapes=[pltpu.VMEM((SUBLANES, LANES), F32)],
        compiler_params=_cparams(2, 44 << 20),
    )(h, wq, wk, wv, wf, fb, gq, gk, *alias_in)


def _lambda(lq1_ref, lk1_ref, lq2_ref, lk2_ref, lam_init):
    a = jnp.sum(lq1_ref[...] * lk1_ref[...], axis=-1, keepdims=True)
    b = jnp.sum(lq2_ref[...] * lk2_ref[...], axis=-1, keepdims=True)
    return jnp.exp(a) - jnp.exp(b) + lam_init


def _diff_combine(a1, l1, a2, l2, lam, g_sub, lam_init):
    y = a1 / l1 - lam * (a2 / l2)
    y = y * lax.rsqrt(jnp.mean(y * y, axis=-1, keepdims=True) + EPS) * g_sub
    return y * (1.0 - lam_init)


SOFTMAX_BANDS = 4


def _col_reduce(x, reduce_fn, combine_fn):
    band = x.shape[0] // SOFTMAX_BANDS
    parts = [reduce_fn(x[r * band:(r + 1) * band], axis=0, keepdims=True) for r in range(SOFTMAX_BANDS)]
    while len(parts) > 1:
        parts = [combine_fn(parts[2 * r], parts[2 * r + 1]) for r in range(len(parts) // 2)]
    return parts[0]


def _flash_t(i, tk, k_ref, vt_ref, qt, q_of_col, m_ref, acc_ref, sa_ref, sb_ref):
    m_ref[...] = jnp.full(m_ref.shape, NEG_INF, F32)
    acc_ref[...] = jnp.zeros(acc_ref.shape, F32)
    ones = jnp.ones((BF16_ROWS, tk), BF16)

    def scores(j, s_ref):
        start = pl.multiple_of(j * tk, tk)
        s_ref[...] = _dot(k_ref[pl.ds(start, tk), :], qt)

    def soft(j, s_ref, masked):
        start = pl.multiple_of(j * tk, tk)
        s = s_ref[...]
        if masked:
            key = lax.broadcasted_iota(jnp.int32, s.shape, 0)
            s = jnp.where(key <= q_of_col, s, NEG_INF)
        m_old = m_ref[...]
        m_new = jnp.maximum(m_old, _col_reduce(s, jnp.max, jnp.maximum))
        alpha = jnp.exp2(m_old - m_new)
        p = jnp.exp2(s - m_new)
        vt = jnp.concatenate([vt_ref[:, pl.ds(start, tk)], ones], axis=0)
        acc_ref[...] = alpha * acc_ref[...] + _dot(vt, p.astype(BF16))
        m_ref[...] = m_new

    scores(0, sa_ref)

    def body(jj, c):
        j = 2 * jj
        scores(j + 1, sb_ref)
        soft(j, sa_ref, False)
        scores(j + 2, sa_ref)
        soft(j + 1, sb_ref, False)
        return c

    lax.fori_loop(0, lax.shift_right_logical(i, 1), body, 0)
    odd = (i & 1) == 1

    @pl.when(odd)
    def _():
        scores(i, sb_ref)
        soft(i - 1, sa_ref, False)
        soft(i, sb_ref, True)

    @pl.when(jnp.logical_not(odd))
    def _():
        soft(i, sa_ref, True)


def _normalized(acc_ref):
    return acc_ref[0:HEAD_DIM, :] / acc_ref[HEAD_DIM:HEAD_DIM + 1, :]


def _diff_attn_kernel(q_ref, k_ref, vt_ref, lq1_ref, lk1_ref, lq2_ref, lk2_ref, gs_ref, o_ref,
                      m_ref, acc_ref, sa_ref, sb_ref, *, tq, lam_init):
    qt = q_ref[...].astype(F32).T
    sub = lax.broadcasted_iota(jnp.int32, qt.shape, 0)
    qt2 = jnp.concatenate([jnp.where(sub < DK_DIFF, qt, 0.0), jnp.where(sub < DK_DIFF, 0.0, qt)],
                          axis=1).astype(BF16)
    col = lax.broadcasted_iota(jnp.int32, (1, 2 * tq), 1) & (tq - 1)
    _flash_t(pl.program_id(2), tq, k_ref, vt_ref, qt2, col, m_ref, acc_ref, sa_ref, sb_ref)
    lam = _lambda(lq1_ref, lk1_ref, lq2_ref, lk2_ref, lam_init)
    o = _normalized(acc_ref)
    y = (o[:, :tq] - lam * o[:, tq:]).T
    y = y * lax.rsqrt(jnp.mean(y * y, axis=-1, keepdims=True) + EPS) * gs_ref[...]
    o_ref[...] = (y * (1.0 - lam_init)).astype(o_ref.dtype)


def _fox_attn_kernel(q_ref, k_ref, vt_ref, o_ref, m_ref, acc_ref, sa_ref, sb_ref, *, tq):
    qt = q_ref[...].astype(F32).T
    sub = lax.broadcasted_iota(jnp.int32, qt.shape, 0)
    ones = jnp.where(sub < BIAS_PARTS, 1.0, 0.0)
    qa = jnp.concatenate([qt, ones], axis=0).astype(BF16)
    col = lax.broadcasted_iota(jnp.int32, (1, tq), 1)
    _flash_t(pl.program_id(2), tq, k_ref, vt_ref, qa, col, m_ref, acc_ref, sa_ref, sb_ref)
    o_ref[...] = _normalized(acc_ref).T.astype(o_ref.dtype)


def _attn_call(kern, name, q, k, vt, extra, extra_specs, nb, t, tq, ncols):
    assert tq & (tq - 1) == 0 and tq % (SOFTMAX_BANDS * SUBLANES) == 0
    nt = t // tq
    kd = k.shape[-1]
    qspec = pl.BlockSpec((tq, HEAD_DIM), lambda b, h, i: (b * nt + i, h))
    score = pltpu.VMEM((tq, ncols), F32)
    return pl.pallas_call(
        kern, name=name,
        grid=(nb, N_HEADS, nt),
        in_specs=[qspec,
                  pl.BlockSpec((None, None, t, kd), lambda b, h, i: (b, h, 0, 0)),
                  pl.BlockSpec((None, None, HEAD_DIM, t), lambda b, h, i: (b, h, 0, 0))] + extra_specs,
        out_specs=qspec,
        out_shape=jax.ShapeDtypeStruct(q.shape, BF16),
        scratch_shapes=[pltpu.VMEM((1, ncols), F32), pltpu.VMEM((HEAD_DIM + BF16_ROWS, ncols), F32),
                        score, score],
        compiler_params=_cparams(3, 40 << 20),
    )(q, k, vt, *extra)


def _diff_attn(q, k, vt, lq1, lk1, lq2, lk2, g_sub, nb, t, tq, lam_init):
    vec = pl.BlockSpec((1, DK_DIFF), lambda b, h, i: (0, 0))
    return _attn_call(functools.partial(_diff_attn_kernel, tq=tq, lam_init=lam_init), "diff_attn", q, k, vt,
                      [lq1, lk1, lq2, lk2, g_sub],
                      [vec, vec, vec, vec, pl.BlockSpec((1, HEAD_DIM), lambda b, h, i: (0, 0))],
                      nb, t, tq, 2 * tq)


def _fox_attn(q, k, vt, nb, t, tq):
    return _attn_call(functools.partial(_fox_attn_kernel, tq=tq), "fox_attn", q, k, vt, [], [], nb, t, tq, tq)


def _head_rows(q, width, n_groups):
    qt = jnp.concatenate([q] * n_groups, axis=0)
    lane = lax.broadcasted_iota(jnp.int32, qt.shape, 1)
    grp = jnp.right_shift(lax.broadcasted_iota(jnp.int32, qt.shape, 0), 3)
    lo = grp * width
    return jnp.where((lane >= lo) & (lane < lo + width), qt, 0.0).astype(BF16)


def _expand_heads(x, n):
    return jnp.concatenate([jnp.broadcast_to(x[h:h + 1, :], (SUBLANES, n)) for h in range(N_HEADS)], axis=0)


def _flat_heads(ref):
    return jnp.concatenate([ref[h].astype(BF16) for h in range(N_HEADS)], axis=1)


def _dec_update(s, v_tiles, m_ref, l_ref, acc_ref):
    m_old = m_ref[...]
    m_new = jnp.maximum(m_old, jnp.max(s, axis=-1, keepdims=True))
    alpha = jnp.exp(m_old - m_new)
    p = jnp.exp(s - m_new)
    l_ref[...] = alpha * l_ref[...] + jnp.sum(p, axis=-1, keepdims=True)
    pb = p.astype(BF16)
    pv = None
    for r, v in enumerate(v_tiles):
        d = _dot(pb[:, r * LANES:(r + 1) * LANES], v)
        pv = d if pv is None else pv + d
    acc_ref[...] = alpha * acc_ref[...] + pv
    m_ref[...] = m_new


def _new_tokens(qr, kn_ref, vn_ref, bias_rows):
    def padded(ref):
        x = jnp.concatenate([ref[h] for h in range(N_HEADS)], axis=1)
        return jnp.concatenate([x, jnp.zeros((LANES - SUBLANES, ATT_W), F32)], axis=0).astype(BF16)

    s = _dot_nt(qr, padded(kn_ref))
    if bias_rows is not None:
        s = s + bias_rows
    keep = (lax.broadcasted_iota(jnp.int32, s.shape, 1)
            <= (lax.broadcasted_iota(jnp.int32, s.shape, 0) & (SUBLANES - 1)))
    return jnp.where(keep, s, NEG_INF), padded(vn_ref)


def _dec_init(q_ref, qr_ref, m_ref, l_ref, acc_ref, width, n_groups):
    qr_ref[...] = _head_rows(q_ref[...], width, n_groups)
    m_ref[...] = jnp.full_like(m_ref, NEG_INF)
    l_ref[...] = jnp.zeros_like(l_ref)
    acc_ref[...] = jnp.zeros_like(acc_ref)


def _diff_dec_kernel(pt_ref, q_ref, kn_ref, vn_ref, lq1_ref, lk1_ref, lq2_ref, lk2_ref, gs_ref, *rest,
                     npg, lam_init):
    k_refs = rest[:npg]
    v_refs = rest[npg:2 * npg]
    o_ref, qr_ref, m_ref, l_ref, acc_ref = rest[2 * npg:]
    j = pl.program_id(1)

    @pl.when(j == 0)
    def _():
        _dec_init(q_ref, qr_ref, m_ref, l_ref, acc_ref, DK_DIFF, 2 * N_HEADS)

    qr = qr_ref[...]
    s = jnp.concatenate([_dot_nt(qr, _flat_heads(k)) for k in k_refs], axis=1)
    _dec_update(s, [_flat_heads(v) for v in v_refs], m_ref, l_ref, acc_ref)

    @pl.when(j == pl.num_programs(1) - 1)
    def _():
        sn, vn = _new_tokens(qr, kn_ref, vn_ref, None)
        _dec_update(sn, [vn], m_ref, l_ref, acc_ref)
        lam = _lambda(lq1_ref, lk1_ref, lq2_ref, lk2_ref, lam_init)
        acc = acc_ref[...]
        l = l_ref[...]
        for h in range(N_HEADS):
            r1 = slice(2 * h * SUBLANES, (2 * h + 1) * SUBLANES)
            r2 = slice((2 * h + 1) * SUBLANES, (2 * h + 2) * SUBLANES)
            cl = slice(h * HEAD_DIM, (h + 1) * HEAD_DIM)
            o_ref[:, cl] = _diff_combine(acc[r1, cl], l[r1], acc[r2, cl], l[r2], lam, gs_ref[...], lam_init)


def _fox_dec_kernel(pt_ref, q_ref, kn_ref, vn_ref, bn_ref, tri_ref, *rest, npg):
    k_refs = rest[:npg]
    v_refs = rest[npg:2 * npg]
    lf_refs = rest[2 * npg:3 * npg]
    o_ref, qr_ref, m_ref, l_ref, acc_ref, carry_ref = rest[3 * npg:]
    j = pl.program_id(1)

    @pl.when(j == 0)
    def _():
        _dec_init(q_ref, qr_ref, m_ref, l_ref, acc_ref, HEAD_DIM, N_HEADS)
        carry_ref[...] = jnp.zeros_like(carry_ref)

    terms = []
    for lf in lf_refs:
        terms += list(_split3(lf[...]))
    y = _dot(jnp.concatenate(terms, axis=0).astype(BF16), tri_ref[...])
    c = carry_ref[...]
    biases = []
    rows = BIAS_PARTS * SUBLANES
    for r in range(npg):
        yr = y[r * rows:(r + 1) * rows]
        blk = (yr[0:SUBLANES] + yr[SUBLANES:2 * SUBLANES]) + yr[2 * SUBLANES:3 * SUBLANES] + c
        biases.append(-blk)
        c = jnp.broadcast_to(blk[:, LANES - 1:LANES], blk.shape)
    carry_ref[...] = c

    qr = qr_ref[...]
    s = jnp.concatenate([_dot_nt(qr, _flat_heads(k)) for k in k_refs], axis=1)
    s = s + _expand_heads(jnp.concatenate(biases, axis=1), npg * LANES)
    _dec_update(s, [_flat_heads(v) for v in v_refs], m_ref, l_ref, acc_ref)

    @pl.when(j == pl.num_programs(1) - 1)
    def _():
        sn, vn = _new_tokens(qr, kn_ref, vn_ref, _expand_heads(bn_ref[...] - c, LANES))
        _dec_update(sn, [vn], m_ref, l_ref, acc_ref)
        acc = acc_ref[...]
        l = l_ref[...]
        for h in range(N_HEADS):
            rs = slice(h * SUBLANES, (h + 1) * SUBLANES)
            cl = slice(h * HEAD_DIM, (h + 1) * HEAD_DIM)
            o_ref[:, cl] = acc[rs, cl] / l[rs]


def _page_specs(layer, n_pages, npg, block):
    nz = (0,) * (len(block) - 2)

    def spec(r):
        return pl.BlockSpec(block, lambda b, j, pt, r=r: (layer, pt[b * n_pages + j * npg + r]) + nz)
    return [spec(r) for r in range(npg)]


def _dec_common(layer, tnew, rows):
    tok = pl.BlockSpec((tnew, ATT_W), lambda b, j, pt: (b, 0))
    new_kv = pl.BlockSpec((None, None, N_HEADS, tnew, HEAD_DIM), lambda b, j, pt: (layer, b, 0, 0, 0))
    scratch = [pltpu.VMEM((rows, ATT_W), BF16), pltpu.VMEM((rows, 1), F32), pltpu.VMEM((rows, 1), F32),
               pltpu.VMEM((rows, ATT_W), F32)]
    return tok, new_kv, scratch


def _diff_decode(pt, q, kn, vn, lq1, lk1, lq2, lk2, g_sub, cache_k, cache_v, layer, nb, n_pages, lam_init):
    tnew = q.shape[0] // nb
    assert tnew == SUBLANES
    page = cache_k.shape[3]
    npg = _tile(n_pages, DEC_PAGES_PER_STEP)
    tok, new_kv, scratch = _dec_common(layer, tnew, 2 * N_HEADS * SUBLANES)
    vec = pl.BlockSpec((1, DK_DIFF), lambda b, j, pt: (0, 0))
    pages = _page_specs(layer, n_pages, npg, (None, None, N_HEADS, page, HEAD_DIM))
    return pl.pallas_call(
        functools.partial(_diff_dec_kernel, npg=npg, lam_init=lam_init), name="diff_decode",
        grid_spec=pltpu.PrefetchScalarGridSpec(
            num_scalar_prefetch=1, grid=(nb, n_pages // npg),
            in_specs=[tok, new_kv, new_kv, vec, vec, vec, vec,
                      pl.BlockSpec((1, HEAD_DIM), lambda b, j, pt: (0, 0))] + pages + pages,
            out_specs=tok, scratch_shapes=scratch),
        out_shape=jax.ShapeDtypeStruct(q.shape, F32),
        compiler_params=_cparams(2, 4 * npg * math.prod(cache_k.shape[2:]) * 4 + (20 << 20)),
    )(pt, q, kn, vn, lq1, lk1, lq2, lk2, g_sub, *([cache_k] * npg), *([cache_v] * npg))


def _fox_decode(pt, q, kn, vn, bias_new, cache_k, cache_v, cache_lf, layer, nb, n_pages):
    tnew = q.shape[0] // nb
    assert tnew == SUBLANES
    page = cache_k.shape[3]
    assert page == LANES
    npg = _tile(n_pages, DEC_PAGES_PER_STEP)
    tok, new_kv, scratch = _dec_common(layer, tnew, N_HEADS * SUBLANES)
    pages = _page_specs(layer, n_pages, npg, (None, None, N_HEADS, page, HEAD_DIM))
    lf_pages = _page_specs(layer, n_pages, npg, (None, None, SUBLANES, page))
    idx = jnp.arange(page, dtype=jnp.int32)
    tri = (idx[:, None] <= idx[None, :]).astype(BF16)
    return pl.pallas_call(
        functools.partial(_fox_dec_kernel, npg=npg), name="fox_decode",
        grid_spec=pltpu.PrefetchScalarGridSpec(
            num_scalar_prefetch=1, grid=(nb, n_pages // npg),
            in_specs=[tok, new_kv, new_kv,
                      pl.BlockSpec((None, SUBLANES, LANES), lambda b, j, pt: (b, 0, 0)),
                      pl.BlockSpec((page, page), lambda b, j, pt: (0, 0))]
            + pages + pages + lf_pages,
            out_specs=tok, scratch_shapes=scratch + [pltpu.VMEM((SUBLANES, LANES), F32)]),
        out_shape=jax.ShapeDtypeStruct(q.shape, F32),
        compiler_params=_cparams(2, 4 * npg * math.prod(cache_k.shape[2:]) * 4 + (20 << 20)),
    )(pt, q, kn, vn, bias_new, tri, *([cache_k] * npg), *([cache_v] * npg), *([cache_lf] * npg))


def _out_proj_kernel(yc_ref, yd_ref, yf_ref, w_ref, x_ref, g1_ref, gn_ref, sc_ref, sh_ref, x1_ref, h2_ref):
    mix = jnp.concatenate([yc_ref[...].astype(BF16), yd_ref[...].astype(BF16), yf_ref[...].astype(BF16)], axis=1)
    x1 = x_ref[...] + g1_ref[...] * _dot(mix, w_ref[...])
    x1_ref[...] = x1
    h2_ref[...] = _norm_mod_value(x1, gn_ref[...], sc_ref[...], sh_ref[...]).astype(h2_ref.dtype)


def _out_proj(yc, yd, yf, w_out, layer, x, g1, gn, sc2, sh2, tm, tiles_per_group):
    m, d = x.shape
    row = lambda i: (i, 0)
    return pl.pallas_call(
        _out_proj_kernel, name="out_proj",
        grid=(m // tm,),
        in_specs=[pl.BlockSpec((tm, C_CONV), row), pl.BlockSpec((tm, ATT_W), row), pl.BlockSpec((tm, ATT_W), row),
                  _layer_spec(w_out, layer), pl.BlockSpec((tm, d), row), _mod_spec(g1, tiles_per_group),
                  _full_spec((1, d)), _mod_spec(sc2, tiles_per_group), _mod_spec(sh2, tiles_per_group)],
        out_specs=[pl.BlockSpec((tm, d), row), pl.BlockSpec((tm, d), row)],
        out_shape=[jax.ShapeDtypeStruct((m, d), F32), jax.ShapeDtypeStruct((m, d), BF16)],
        compiler_params=_cparams(1, 40 << 20),
    )(yc, yd, yf, w_out, x, g1, gn, sc2, sh2)


def _ffn_up_kernel(h_ref, wa_ref, wb_ref, g_ref):
    hb = h_ref[...]
    a = _dot(hb, wa_ref[...])
    b = _dot(hb, wb_ref[...])
    g_ref[...] = (a * jax.nn.sigmoid(a) * b).astype(g_ref.dtype)


def _ffn_up(h2, w_in, layer, tm, tf):
    m, d = h2.shape
    f = w_in.shape[2] // 2
    nf = f // tf
    return pl.pallas_call(
        _ffn_up_kernel, name="ffn_up",
        grid=(m // tm, nf),
        in_specs=[pl.BlockSpec((tm, d), lambda i, j: (i, 0)),
                  pl.BlockSpec((None, d, tf), lambda i, j: (layer, 0, j)),
                  pl.BlockSpec((None, d, tf), lambda i, j: (layer, 0, j + nf))],
        out_specs=pl.BlockSpec((tm, tf), lambda i, j: (i, j)),
        out_shape=jax.ShapeDtypeStruct((m, f), BF16),
        compiler_params=_cparams(2, 40 << 20),
    )(h2, w_in, w_in)


def _ffn_down_kernel(g_ref, w_ref, x_ref, g2_ref, *rest, next_norm):
    if next_norm:
        gn_ref, sc_ref, sh_ref, x2_ref, hn_ref, acc_ref = rest
    else:
        x2_ref, acc_ref = rest
    k = pl.program_id(1)

    @pl.when(k == 0)
    def _():
        acc_ref[...] = jnp.zeros_like(acc_ref)

    acc_ref[...] += _dot(g_ref[...], w_ref[...])

    @pl.when(k == pl.num_programs(1) - 1)
    def _():
        x2 = x_ref[...] + g2_ref[...] * acc_ref[...]
        x2_ref[...] = x2
        if next_norm:
            hn_ref[...] = _norm_mod_value(x2, gn_ref[...], sc_ref[...], sh_ref[...]).astype(hn_ref.dtype)


def _ffn_down(g, w_out, layer, x1, g2, nxt, tm, tk, tiles_per_group):
    m, f = g.shape
    d = w_out.shape[2]
    xspec = pl.BlockSpec((tm, d), lambda i, k: (i, 0))
    in_specs = [pl.BlockSpec((tm, tk), lambda i, k: (i, k)),
                pl.BlockSpec((None, tk, d), lambda i, k: (layer, k, 0)),
                xspec, _mod_spec(g2, tiles_per_group)]
    args = [g, w_out, x1, g2]
    out_specs, out_shape = [xspec], [jax.ShapeDtypeStruct((m, d), F32)]
    if nxt is not None:
        in_specs += [_full_spec((1, d)), _mod_spec(nxt[1], tiles_per_group), _mod_spec(nxt[2], tiles_per_group)]
        args += list(nxt)
        out_specs.append(xspec)
        out_shape.append(jax.ShapeDtypeStruct((m, d), BF16))
    out = pl.pallas_call(
        functools.partial(_ffn_down_kernel, next_norm=nxt is not None), name="ffn_down",
        grid=(m // tm, f // tk),
        in_specs=in_specs, out_specs=out_specs, out_shape=out_shape,
        scratch_shapes=[pltpu.VMEM((tm, d), F32)],
        compiler_params=_cparams(2, 48 << 20),
    )(*args)
    return (out[0], out[1]) if nxt is not None else (out[0], None)


def _rope_tables(pos):
    half = DK_DIFF // 2
    inv = ROPE_THETA ** (-jnp.arange(half, dtype=F32) / half)
    ang = pos.astype(F32)[:, None] * inv[None, :]
    cos = jnp.tile(jnp.cos(ang), (1, LANES // half))
    sin = jnp.sin(ang)
    sin = jnp.tile(jnp.concatenate([-sin, sin], axis=1), (1, LANES // DK_DIFF))
    return cos, sin


class _Group:
    def __init__(self, nb, t, is_prompt):
        self.nb, self.t, self.is_prompt = nb, t, is_prompt
        self.m = nb * t
        if is_prompt:
            self.tm = _tile(t, 256)
            self.tm_conv = _tile(t, 512)
            self.tm_ffn = _tile(t, 1024)
            self.tm_down = _tile(t, 512)
            self.tq = _tile(t, 512)
            self.down_tiles = t // self.tm_down
            self.tiles = t // self.tm
        else:
            self.tm = self.tm_conv = self.tm_ffn = self.tm_down = self.m
            self.down_tiles = self.tiles = 1


def _mods(ada, grp):
    d = ada.shape[1] // 6
    parts = jnp.split(ada, 6, axis=-1)
    if grp.is_prompt:
        return [p.reshape(grp.nb, 1, d) for p in parts]
    return [jnp.repeat(p, grp.t, axis=0).reshape(1, grp.m, d) for p in parts]


def _layer(x, h, grp, wts, lw, layer, depth, mods, nxt, rope, lam_init, states, ctx):
    sh1, sc1, g1, sh2, sc2, g2 = mods
    cos, sin = rope
    nb, t = grp.nb, grp.t
    prompt = grp.is_prompt
    prev_d = None if states is None else states[0:2]
    prev_f = None if states is None else states[2:4]

    yc, conv_new = _proj_conv(h, wts["w_cv"], wts["w_cg"], layer, lw["conv_w"], lw["conv_b"], lw["conv_ln_g"],
                              lw["conv_ln_b"], None if prompt else ctx["state_pad"],
                              nb, t, grp.tm_conv, BF16 if prompt else F32)
    dres = _proj_diff(h, wts["w_dq"], wts["w_dk"], wts["w_dv"], layer, depth, lw["diff_qn"], lw["diff_kn"],
                      cos, sin, prev_d, nb, t, grp.tm, prompt)
    fres = _proj_fox(h, wts["w_fq"], wts["w_fk"], wts["w_fv"], wts["w_ff"], layer, depth, lw["fox_fb"],
                     lw["fox_qn"], lw["fox_kn"], prev_f, nb, t, grp.tm, prompt)
    qd, kd, vd = dres[:3]
    qf, kf, vf, lft, fbias = fres[:5]
    if prompt:
        yd = _diff_attn(qd, dres[3], dres[4], lw["lam_q1"], lw["lam_k1"], lw["lam_q2"], lw["lam_k2"],
                        lw["diff_subln"], nb, t, grp.tq, lam_init)
        yf = _fox_attn(qf, fres[5], fres[6], nb, t, grp.tq)
    else:
        yd = _diff_decode(ctx["pt"], qd, kd, vd, lw["lam_q1"], lw["lam_k1"], lw["lam_q2"], lw["lam_k2"],
                          lw["diff_subln"], ctx["cache_dk"], ctx["cache_dv"], layer, nb, ctx["n_pages"], lam_init)
        bn = jnp.transpose(fbias[0, :, :nb * t].reshape(SUBLANES, nb, t), (1, 0, 2))
        bn = jnp.pad(bn, ((0, 0), (0, 0), (0, LANES - t)))
        yf = _fox_decode(ctx["pt"], qf, kf, vf, bn, ctx["cache_fk"], ctx["cache_fv"], ctx["cache_lf"],
                         layer, nb, ctx["n_pages"])
    x1, h2 = _out_proj(yc, yd, yf, wts["w_out"], layer, x, g1, lw["norm_ffn"], sc2, sh2, grp.tm, grp.tiles)
    g = _ffn_up(h2, wts["w_ffn_in"], layer, grp.tm_ffn, wts["tf"])
    x2, h_next = _ffn_down(g, wts["w_ffn_out"], layer, x1, g2, nxt, grp.tm_down, wts["tk"], grp.down_tiles)
    return x2, h_next, (kd, vd, kf, vf), lft, conv_new


def kernel(x_prompt, x_sample, cache_diff_k, cache_diff_v, cache_fox_k, cache_fox_v, cache_fox_lf, state_conv,
           page_table, c_prompt, c_sample, norm_mix, norm_ffn, w_ada, b_ada, w_in, conv_w, conv_b, conv_ln_g,
           conv_ln_b, diff_qn, diff_kn, lam_q1, lam_k1, lam_q2, lam_k2, diff_subln, fox_qn, fox_kn, fox_fb,
           w_out, w_ffn_in, w_ffn_out):
    depth = w_in.shape[0]
    bp, tp, d = x_prompt.shape
    bs, ts, _ = x_sample.shape
    page = cache_diff_k.shape[2]
    n_pages = page_table.shape[1]
    d_ff = w_ffn_out.shape[1]
    assert d_ff % LANES == 0
    gp = _Group(bp, tp, True)
    gs = _Group(bs, ts, False)

    c_all = jnp.concatenate([c_prompt, c_sample], axis=0)
    n_c = c_all.shape[0]
    c_all = jnp.pad(c_all, ((0, (-n_c) % (2 * SUBLANES)), (0, 0)))
    ada = _ada(c_all, w_ada, b_ada)
    mods_p = [_mods(ada[l, :bp], gp) for l in range(depth)]
    mods_s = [_mods(ada[l, bp:bp + bs], gs) for l in range(depth)]

    cuts = [0, C_CONV, 2 * C_CONV]
    for _ in range(6):
        cuts.append(cuts[-1] + ATT_W)
    names = ["w_cv", "w_cg", "w_dq", "w_dk", "w_dv", "w_fq", "w_fk", "w_fv"]
    wts = {n: w_in[:, :, cuts[i]:cuts[i + 1]].astype(BF16) for i, n in enumerate(names)}
    wts["w_ff"] = jnp.pad(w_in[:, :, cuts[-1]:], ((0, 0), (0, 0), (0, LANES - N_HEADS))).astype(BF16)
    wts["w_out"] = w_out.astype(BF16)
    wts["w_ffn_in"] = w_ffn_in.astype(BF16)
    wts["w_ffn_out"] = w_ffn_out.astype(BF16)
    wts["tf"] = _tile(d_ff, 512)
    wts["tk"] = d_ff // 4 if (d_ff // 4) % LANES == 0 else _tile(d_ff, 512)

    rope_p = _rope_tables(jnp.arange(tp, dtype=jnp.int32))
    cos_s, sin_s = _rope_tables(n_pages * page + jnp.arange(ts, dtype=jnp.int32))
    rope_s = (jnp.tile(cos_s, (bs, 1)), jnp.tile(sin_s, (bs, 1)))

    head_major = lambda c: jnp.transpose(c, (0, 1, 3, 2, 4))
    ctx = dict(pt=page_table.reshape(-1).astype(jnp.int32), n_pages=n_pages,
               cache_dk=head_major(cache_diff_k), cache_dv=head_major(cache_diff_v),
               cache_fk=head_major(cache_fox_k), cache_fv=head_major(cache_fox_v),
               cache_lf=jnp.pad(jnp.transpose(cache_fox_lf, (0, 1, 3, 2)),
                                ((0, 0), (0, 0), (0, SUBLANES - N_HEADS), (0, 0))))
    state_pad = jnp.pad(state_conv, ((0, 0), (0, 0), (HIST_PAD - CONV_HIST, 0), (0, 0)))

    xp = x_prompt.reshape(bp * tp, d)
    xs = x_sample.reshape(bs * ts, d)
    g0 = norm_mix[0][None, :]
    hp = _norm_mod(xp, g0, mods_p[0][1], mods_p[0][0], gp.tm_down, gp.down_tiles)
    hs = _norm_mod(xs, g0, mods_s[0][1], mods_s[0][0], gs.tm, 1)
    kv_p = kv_s = None
    lf_p, lf_s, conv_p, conv_s = [], [], [], []
    for l in range(depth):
        lam_init = 0.8 - 0.6 * math.exp(-0.3 * l)
        lw = dict(
            conv_w=conv_w[l], conv_b=conv_b[l][None, :], conv_ln_g=conv_ln_g[l][None, :],
            conv_ln_b=conv_ln_b[l][None, :],
            diff_qn=jnp.tile(diff_qn[l], LANES // DK_DIFF)[None, :],
            diff_kn=jnp.tile(diff_kn[l], LANES // DK_DIFF)[None, :],
            lam_q1=lam_q1[l][None, :], lam_k1=lam_k1[l][None, :], lam_q2=lam_q2[l][None, :],
            lam_k2=lam_k2[l][None, :], diff_subln=diff_subln[l][None, :],
            fox_qn=fox_qn[l][None, :], fox_kn=fox_kn[l][None, :],
            fox_fb=jnp.pad(fox_fb[l], (0, LANES - N_HEADS))[None, :],
            norm_ffn=norm_ffn[l][None, :])
        last = l == depth - 1
        nxt_p = None if last else (norm_mix[l + 1][None, :], mods_p[l + 1][1], mods_p[l + 1][0])
        nxt_s = None if last else (norm_mix[l + 1][None, :], mods_s[l + 1][1], mods_s[l + 1][0])
        xp, hp, kv_p, lft, cn = _layer(xp, hp, gp, wts, lw, l, depth, mods_p[l], nxt_p, rope_p, lam_init, kv_p,
                                       None)
        lf_p.append(lft)
        conv_p.append(cn)
        xs, hs, kv_s, lft, cn = _layer(xs, hs, gs, wts, lw, l, depth, mods_s[l], nxt_s, rope_s, lam_init, kv_s,
                                       dict(ctx, state_pad=state_pad[l]))
        lf_s.append(lft)
        conv_s.append(cn)

    kv = lambda a: jnp.transpose(a, (0, 1, 3, 2, 4))
    lf_p = jnp.transpose(jnp.stack(lf_p, axis=0)[:, :, :N_HEADS, :], (0, 1, 3, 2))
    lf_s = jnp.stack(lf_s, axis=0)[:, 0, :N_HEADS, :bs * ts]
    lf_s = jnp.transpose(lf_s.reshape(depth, N_HEADS, bs, ts), (0, 2, 3, 1))
    return (xp.reshape(bp, tp, d), xs.reshape(bs, ts, d),
            kv(kv_p[0]), kv(kv_p[1]), kv(kv_p[2]), kv(kv_p[3]), lf_p, jnp.stack(conv_p, axis=0),
            kv(kv_s[0]), kv(kv_s[1]), kv(kv_s[2]), kv(kv_s[3]), lf_s, jnp.stack(conv_s, axis=0))
```

```python
import functools
import math

import jax
import jax.numpy as jnp
from jax import lax
from jax.experimental import pallas as pl
from jax.experimental.pallas import tpu as pltpu

F32 = jnp.float32
BF16 = jnp.bfloat16

EPS = 1e-6
NEG_INF = -1e30
ROPE_THETA = 10000.0
C_CONV = 512
CONV_WIDTH = 31
CONV_HIST = CONV_WIDTH - 1
N_HEADS = 6
HEAD_DIM = 128
DK_DIFF = 64
ATT_W = N_HEADS * HEAD_DIM

LANES = 128
SUBLANES = 8
V7X_VMEM_BYTES = 64 * 1024 * 1024
VMEM_CAP = V7X_VMEM_BYTES - 8 * 1024 * 1024
HIST_PAD = 32
CONV_CHUNK = 64
BIAS_PARTS = 3
LOG2E = 1.4426950408889634
BF16_ROWS = 16

DEC_PAGES_PER_STEP = 16


def _dot(a, b):
    return jnp.dot(a, b, preferred_element_type=F32)


def _dot_nt(a, b):
    return lax.dot_general(a, b, (((1,), (1,)), ((), ())), preferred_element_type=F32)


def _cparams(n_grid, vmem_bytes):
    return pltpu.CompilerParams(dimension_semantics=("arbitrary",) * n_grid,
                                vmem_limit_bytes=int(min(max(vmem_bytes, 16 << 20), VMEM_CAP)))


def _tile(n, pref):
    t = min(n, pref)
    assert n % t == 0, (n, pref)
    return t


def _full_spec(shape):
    nd = len(shape)
    return pl.BlockSpec(shape, lambda *_: (0,) * nd)


def _layer_spec(w, layer):
    return pl.BlockSpec((None,) + tuple(w.shape[1:]), lambda *_: (layer, 0, 0), pipeline_mode=pl.Buffered(1))


def _mod_spec(mod, tiles_per_group):
    _, r, d = mod.shape
    return pl.BlockSpec((None, r, d), lambda i, *_: (i // tiles_per_group, 0, 0))


def _split3(x):
    hi = x.astype(BF16).astype(F32)
    r1 = x - hi
    mid = r1.astype(BF16).astype(F32)
    lo = (r1 - mid).astype(BF16).astype(F32)
    return hi, mid, lo


def _ada_kernel(c_ref, w_ref, b_ref, o_ref):
    c = c_ref[...]
    a = c * jax.nn.sigmoid(c)
    a_hi = a.astype(BF16)
    a_lo = (a - a_hi.astype(F32)).astype(BF16)
    w = w_ref[...]
    w_hi = w.astype(BF16)
    w_lo = (w - w_hi.astype(F32)).astype(BF16)
    o_ref[...] = _dot(a_hi, w_hi) + _dot(a_hi, w_lo) + _dot(a_lo, w_hi) + b_ref[...]


def _ada(c_all, w_ada, b_ada):
    depth, d, n = w_ada.shape
    rows = c_all.shape[0]
    tn = _tile(n, 512)
    return pl.pallas_call(
        _ada_kernel, name="ada",
        grid=(depth, n // tn),
        in_specs=[pl.BlockSpec((rows, d), lambda l, j: (0, 0)),
                  pl.BlockSpec((None, d, tn), lambda l, j: (l, 0, j)),
                  pl.BlockSpec((None, 1, tn), lambda l, j: (l, 0, j))],
        out_specs=pl.BlockSpec((None, rows, tn), lambda l, j: (l, 0, j)),
        out_shape=jax.ShapeDtypeStruct((depth, rows, n), F32),
        compiler_params=_cparams(2, 6 * d * tn * 4),
    )(c_all, w_ada, b_ada.reshape(depth, 1, n))


def _norm_mod_value(x, g, sc, sh):
    y = x * lax.rsqrt(jnp.mean(x * x, axis=-1, keepdims=True) + EPS) * g
    return y * (1.0 + sc) + sh


def _norm_mod_kernel(x_ref, g_ref, sc_ref, sh_ref, h_ref):
    h_ref[...] = _norm_mod_value(x_ref[...], g_ref[...], sc_ref[...], sh_ref[...]).astype(h_ref.dtype)


def _norm_mod(x, g, sc, sh, tm, tiles_per_group):
    m, d = x.shape
    return pl.pallas_call(
        _norm_mod_kernel, name="norm_mod",
        grid=(m // tm,),
        in_specs=[pl.BlockSpec((tm, d), lambda i: (i, 0)), _full_spec((1, d)),
                  _mod_spec(sc, tiles_per_group), _mod_spec(sh, tiles_per_group)],
        out_specs=pl.BlockSpec((tm, d), lambda i: (i, 0)),
        out_shape=jax.ShapeDtypeStruct((m, d), BF16),
        compiler_params=_cparams(1, 8 * tm * d * 4),
    )(x, g, sc, sh)


def _conv_rows(ext_ref, z_ref, cw_ref, cb_ref, lg_ref, lb_ref, r0, n):
    acc = jnp.broadcast_to(cb_ref[...], (n, C_CONV))
    off = HIST_PAD - CONV_HIST
    for r in range(SUBLANES):
        span = n if r == 0 else n + SUBLANES
        z = None
        for a in range((CONV_WIDTH + off) // SUBLANES + 1):
            j = SUBLANES * a + r - off
            if 0 <= j < CONV_WIDTH:
                term = cw_ref[j:j + 1, :] * ext_ref[r0 + SUBLANES * a:r0 + SUBLANES * a + span, :]
                z = term if z is None else z + term
        if r == 0:
            acc = acc + z
        else:
            z_ref[r] = z
            acc = acc + z_ref[r, r:r + n, :]
    mu = jnp.mean(acc, axis=-1, keepdims=True)
    xc = acc - mu
    var = jnp.mean(xc * xc, axis=-1, keepdims=True)
    y = xc * lax.rsqrt(var + EPS) * lg_ref[...] + lb_ref[...]
    return y * jax.nn.sigmoid(y)


def _proj_conv_kernel(*refs, nseq, rows, carry):
    if carry:
        h_ref, wv_ref, wg_ref, cw_ref, cb_ref, lg_ref, lb_ref, yc_ref, cn_ref, ext_ref, z_ref = refs
        st_ref = None
    else:
        h_ref, wv_ref, wg_ref, cw_ref, cb_ref, lg_ref, lb_ref, st_ref, yc_ref, cn_ref, ext_ref, z_ref = refs
    hb = h_ref[...]
    u = _dot(hb, wv_ref[...]) * jax.nn.sigmoid(_dot(hb, wg_ref[...]))
    chunk = min(rows, CONV_CHUNK)
    for s in range(nseq):
        if carry:
            @pl.when(pl.program_id(1) == 0)
            def _():
                ext_ref[0:HIST_PAD, :] = jnp.zeros((HIST_PAD, C_CONV), F32)
        else:
            ext_ref[0:HIST_PAD, :] = st_ref[s]
        ext_ref[HIST_PAD:HIST_PAD + rows, :] = u[s * rows:(s + 1) * rows, :]
        for r0 in range(0, rows, chunk):
            y = _conv_rows(ext_ref, z_ref, cw_ref, cb_ref, lg_ref, lb_ref, r0, chunk)
            yc_ref[s * rows + r0:s * rows + r0 + chunk, :] = y.astype(yc_ref.dtype)
        new_hist = ext_ref[rows + HIST_PAD - CONV_HIST:rows + HIST_PAD, :]
        if carry:
            tail = ext_ref[rows:rows + HIST_PAD, :]
            ext_ref[0:HIST_PAD, :] = tail

            @pl.when(pl.program_id(1) == pl.num_programs(1) - 1)
            def _():
                cn_ref[0] = new_hist
        else:
            cn_ref[s] = new_hist


def _proj_conv(h, w_cv, w_cg, layer, conv_w, conv_b, ln_g, ln_b, state_pad, nb, t, tm, y_dtype):
    m, d = h.shape
    carry = state_pad is None
    small = [_layer_spec(w_cv, layer), _layer_spec(w_cg, layer), _full_spec((CONV_WIDTH, C_CONV)),
             _full_spec((1, C_CONV)), _full_spec((1, C_CONV)), _full_spec((1, C_CONV))]
    if carry:
        assert tm >= HIST_PAD
        nt = t // tm
        grid = (nb, nt)
        in_specs = [pl.BlockSpec((tm, d), lambda b, i: (b * nt + i, 0))] + small
        out_specs = [pl.BlockSpec((tm, C_CONV), lambda b, i: (b * nt + i, 0)),
                     pl.BlockSpec((1, CONV_HIST, C_CONV), lambda b, i: (b, 0, 0))]
        kern = functools.partial(_proj_conv_kernel, nseq=1, rows=tm, carry=True)
        args = (h, w_cv, w_cg, conv_w, conv_b, ln_g, ln_b)
        rows = tm
    else:
        grid = (1, 1)
        in_specs = [_full_spec((m, d))] + small + [_full_spec((nb, HIST_PAD, C_CONV))]
        out_specs = [_full_spec((m, C_CONV)), _full_spec((nb, CONV_HIST, C_CONV))]
        kern = functools.partial(_proj_conv_kernel, nseq=nb, rows=t, carry=False)
        args = (h, w_cv, w_cg, conv_w, conv_b, ln_g, ln_b, state_pad)
        rows = t
    return pl.pallas_call(
        kern, name="proj_conv", grid=grid, in_specs=in_specs, out_specs=out_specs,
        out_shape=[jax.ShapeDtypeStruct((m, C_CONV), y_dtype),
                   jax.ShapeDtypeStruct((nb, CONV_HIST, C_CONV), F32)],
        scratch_shapes=[pltpu.VMEM((HIST_PAD + rows, C_CONV), F32),
                        pltpu.VMEM((SUBLANES, min(rows, CONV_CHUNK) + SUBLANES, C_CONV), F32)],
        compiler_params=_cparams(2, 24 << 20),
    )(*args)


def _store_heads(ref, hh, x, nseq, rows):
    for s in range(nseq):
        ref[s, hh] = x[s * rows:(s + 1) * rows, :].astype(ref.dtype)


def _halfnorm_rope(x, g, cos, sin, lane):
    sq = x * x
    lo = lane < DK_DIFF
    s_lo = jnp.sum(jnp.where(lo, sq, 0.0), axis=-1, keepdims=True)
    s_hi = jnp.sum(jnp.where(lo, 0.0, sq), axis=-1, keepdims=True)
    ms = jnp.where(lo, s_lo, s_hi) * (1.0 / DK_DIFF)
    y = x * lax.rsqrt(ms + EPS) * g
    half = DK_DIFF // 2
    rot = jnp.where((lane & (DK_DIFF - 1)) < half,
                    pltpu.roll(y, LANES - half, 1), pltpu.roll(y, half, 1))
    return y * cos + rot * sin


def _proj_diff_kernel(*refs, nseq, rows, n_alias, attn_ops):
    h_ref, wq_ref, wk_ref, wv_ref, gq_ref, gk_ref, cos_ref, sin_ref = refs[:8]
    outs = refs[8 + n_alias:]
    q_ref, ks_ref, vs_ref = outs[:3]
    hb = h_ref[...]
    tm = hb.shape[0]
    cos = cos_ref[...]
    sin = sin_ref[...]
    lane = lax.broadcasted_iota(jnp.int32, (tm, LANES), 1)
    scale = DK_DIFF ** -0.5 * (LOG2E if attn_ops else 1.0)
    for pair in range(N_HEADS // 2):
        cols = slice(2 * pair * HEAD_DIM, (2 * pair + 2) * HEAD_DIM)
        zq = _dot(hb, wq_ref[:, cols])
        zk = _dot(hb, wk_ref[:, cols])
        zv = _dot(hb, wv_ref[:, cols])
        for sub in range(2):
            hh = 2 * pair + sub
            sl = slice(sub * HEAD_DIM, (sub + 1) * HEAD_DIM)
            q = _halfnorm_rope(zq[:, sl], gq_ref[...], cos, sin, lane) * scale
            q_ref[:, hh * HEAD_DIM:(hh + 1) * HEAD_DIM] = q.astype(q_ref.dtype)
            k = _halfnorm_rope(zk[:, sl], gk_ref[...], cos, sin, lane)
            v = zv[:, sl]
            _store_heads(ks_ref, hh, k, nseq, rows)
            _store_heads(vs_ref, hh, v, nseq, rows)
            if attn_ops:
                kb_ref, vt_ref = outs[3:5]
                kb_ref[0, hh] = k.astype(BF16)
                vt_ref[0, hh] = v.T.astype(BF16)


def _state_out(stack_shape, layer, nseq, rows):
    spec = pl.BlockSpec((None, nseq, N_HEADS, rows, HEAD_DIM), lambda b, i: (layer, b, 0, i, 0))
    return spec, jax.ShapeDtypeStruct(stack_shape, F32)


def _attn_operand_out(nb, t, tm, kd):
    specs = [pl.BlockSpec((1, N_HEADS, tm, kd), lambda b, i: (b, 0, i, 0)),
             pl.BlockSpec((1, N_HEADS, HEAD_DIM, tm), lambda b, i: (b, 0, 0, i))]
    shapes = [jax.ShapeDtypeStruct((nb, N_HEADS, t, kd), BF16),
              jax.ShapeDtypeStruct((nb, N_HEADS, HEAD_DIM, t), BF16)]
    return specs, shapes


def _alias_args(prev, n_in, first_out):
    if prev is None:
        return [], [], {}
    specs = [pl.BlockSpec(memory_space=pl.ANY)] * len(prev)
    return list(prev), specs, {n_in + k: first_out + k for k in range(len(prev))}


def _proj_diff(h, wq, wk, wv, layer, depth, gq, gk, cos, sin, prev, nb, t, tm, is_prompt):
    m, d = h.shape
    if is_prompt:
        nt, grid, nseq, rows = t // tm, (nb, t // tm), 1, tm
    else:
        nt, grid, nseq, rows = 1, (1, 1), nb, t
    row = lambda b, i: (b * nt + i, 0)
    att = pl.BlockSpec((tm, ATT_W), row)
    rope = pl.BlockSpec((tm, LANES), lambda b, i: (i, 0))
    sspec, sshape = _state_out((depth, nb, N_HEADS, t, HEAD_DIM), layer, nseq, rows)
    out_specs = [att, sspec, sspec]
    out_shape = [jax.ShapeDtypeStruct((m, ATT_W), BF16 if is_prompt else F32), sshape, sshape]
    if is_prompt:
        aspecs, ashapes = _attn_operand_out(nb, t, tm, HEAD_DIM)
        out_specs += aspecs
        out_shape += ashapes
    in_specs = [pl.BlockSpec((tm, d), row), _layer_spec(wq, layer), _layer_spec(wk, layer),
                _layer_spec(wv, layer), _full_spec((1, LANES)), _full_spec((1, LANES)), rope, rope]
    alias_in, alias_specs, aliases = _alias_args(prev, len(in_specs), 1)
    return pl.pallas_call(
        functools.partial(_proj_diff_kernel, nseq=nseq, rows=rows, n_alias=len(alias_in), attn_ops=is_prompt),
        name="proj_diff", grid=grid, in_specs=in_specs + alias_specs, out_specs=out_specs, out_shape=out_shape,
        input_output_aliases=aliases,
        compiler_params=_cparams(2, 44 << 20),
    )(h, wq, wk, wv, gq, gk, cos, sin, *alias_in)


def _scan_lanes(x, seg):
    lane = lax.broadcasted_iota(jnp.int32, x.shape, 1)
    pos = lane & (seg - 1)
    s = 1
    while s < seg:
        x = x + jnp.where(pos >= s, pltpu.roll(x, s, 1), 0.0)
        s *= 2
    return x


def _proj_fox_kernel(*refs, nseq, rows, seg, n_alias, attn_ops):
    h_ref, wq_ref, wk_ref, wv_ref, wf_ref, fb_ref, gq_ref, gk_ref = refs[:8]
    outs = refs[8 + n_alias:-1]
    carry_ref = refs[-1]
    q_ref, ks_ref, vs_ref, lf_ref, bias_ref = outs[:5]
    hb = h_ref[...]
    tm = hb.shape[0]
    z = _dot(hb, wf_ref[...]) + fb_ref[...]
    lf = jnp.minimum(z, 0.0) - jnp.log1p(jnp.exp(-jnp.abs(z)))
    if tm < LANES:
        lf = jnp.concatenate([lf, jnp.zeros((LANES - tm, LANES), F32)], axis=0)
    tb = lf.shape[0]
    lft = lf.T[:SUBLANES, :]
    lf_ref[...] = lft
    blocks = []
    if seg >= LANES:
        @pl.when(pl.program_id(1) == 0)
        def _():
            carry_ref[...] = jnp.zeros_like(carry_ref)
        c = carry_ref[...]
        for kb in range(tb // LANES):
            blk = _scan_lanes(lft[:, kb * LANES:(kb + 1) * LANES], LANES) + c
            blocks.append(-blk)
            c = jnp.broadcast_to(blk[:, LANES - 1:LANES], blk.shape)
        carry_ref[...] = c
    else:
        for kb in range(tb // LANES):
            blocks.append(-_scan_lanes(lft[:, kb * LANES:(kb + 1) * LANES], seg))
    bias = jnp.concatenate(blocks, axis=1) if len(blocks) > 1 else blocks[0]
    bias_ref[...] = bias
    if attn_ops:
        bias_t = jnp.concatenate([bias, jnp.zeros((LANES - SUBLANES, tb), F32)], axis=0).T
        lane = lax.broadcasted_iota(jnp.int32, (tm, LANES), 1)

    scale = HEAD_DIM ** -0.5 * (LOG2E if attn_ops else 1.0)
    for pair in range(N_HEADS // 2):
        cols = slice(2 * pair * HEAD_DIM, (2 * pair + 2) * HEAD_DIM)
        zq = _dot(hb, wq_ref[:, cols])
        zk = _dot(hb, wk_ref[:, cols])
        zv = _dot(hb, wv_ref[:, cols])
        for sub in range(2):
            hh = 2 * pair + sub
            sl = slice(sub * HEAD_DIM, (sub + 1) * HEAD_DIM)
            x = zq[:, sl]
            q = x * lax.rsqrt(jnp.mean(x * x, axis=-1, keepdims=True) + EPS) * gq_ref[...] * scale
            q_ref[:, hh * HEAD_DIM:(hh + 1) * HEAD_DIM] = q.astype(q_ref.dtype)
            x = zk[:, sl]
            k = x * lax.rsqrt(jnp.mean(x * x, axis=-1, keepdims=True) + EPS) * gk_ref[...]
            v = zv[:, sl]
            _store_heads(ks_ref, hh, k, nseq, rows)
            _store_heads(vs_ref, hh, v, nseq, rows)
            if attn_ops:
                kb_ref, vt_ref = outs[5:7]
                hi, mid, lo = _split3(jnp.broadcast_to(bias_t[:, hh:hh + 1], (tm, LANES)) * LOG2E)
                aug = jnp.where(lane == 0, hi, jnp.where(lane == 1, mid, jnp.where(lane == 2, lo, 0.0)))
                kb_ref[0, hh, :, 0:HEAD_DIM] = k.astype(BF16)
                kb_ref[0, hh, :, HEAD_DIM:2 * HEAD_DIM] = aug.astype(BF16)
                vt_ref[0, hh] = v.T.astype(BF16)


def _proj_fox(h, wq, wk, wv, wf, layer, depth, fb, gq, gk, prev, nb, t, tm, is_prompt):
    m, d = h.shape
    if is_prompt:
        nt, grid, nseq, rows, ng = t // tm, (nb, t // tm), 1, tm, nb
    else:
        nt, grid, nseq, rows, ng = 1, (1, 1), nb, t, 1
    tb = max(tm, LANES)
    row = lambda b, i: (b * nt + i, 0)
    att = pl.BlockSpec((tm, ATT_W), row)
    tspec = pl.BlockSpec((None, SUBLANES, tb), lambda b, i: (b, 0, i))
    tshape = jax.ShapeDtypeStruct((ng, SUBLANES, nt * tb), F32)
    sspec, sshape = _state_out((depth, nb, N_HEADS, t, HEAD_DIM), layer, nseq, rows)
    out_specs = [att, sspec, sspec, tspec, tspec]
    out_shape = [jax.ShapeDtypeStruct((m, ATT_W), BF16 if is_prompt else F32), sshape, sshape, tshape, tshape]
    if is_prompt:
        aspecs, ashapes = _attn_operand_out(nb, t, tm, 2 * HEAD_DIM)
        out_specs += aspecs
        out_shape += ashapes
    in_specs = [pl.BlockSpec((tm, d), row), _layer_spec(wq, layer), _layer_spec(wk, layer),
                _layer_spec(wv, layer), _layer_spec(wf, layer), _full_spec((1, LANES)),
                _full_spec((1, LANES)), _full_spec((1, LANES))]
    alias_in, alias_specs, aliases = _alias_args(prev, len(in_specs), 1)
    return pl.pallas_call(
        functools.partial(_proj_fox_kernel, nseq=nseq, rows=rows, seg=t, n_alias=len(alias_in),
                          attn_ops=is_prompt),
        name="proj_fox", grid=grid, in_specs=in_specs + alias_specs, out_specs=out_specs, out_shape=out_shape,
        input_output_aliases=aliases,
        scratch_shapes=[pltpu.VMEM((SUBLANES, LANES), F32)],
        compiler_params=_cparams(2, 44 << 20),
    )(h, wq, wk, wv, wf, fb, gq, gk, *alias_in)


def _lambda(lq1_ref, lk1_ref, lq2_ref, lk2_ref, lam_init):
    a = jnp.sum(lq1_ref[...] * lk1_ref[...], axis=-1, keepdims=True)
    b = jnp.sum(lq2_ref[...] * lk2_ref[...], axis=-1, keepdims=True)
    return jnp.exp(a) - jnp.exp(b) + lam_init


def _diff_combine(a1, l1, a2, l2, lam, g_sub, lam_init):
    y = a1 / l1 - lam * (a2 / l2)
    y = y * lax.rsqrt(jnp.mean(y * y, axis=-1, keepdims=True) + EPS) * g_sub
    return y * (1.0 - lam_init)


SOFTMAX_BANDS = 4


def _col_reduce(x, reduce_fn, combine_fn):
    band = x.shape[0] // SOFTMAX_BANDS
    parts = [reduce_fn(x[r * band:(r + 1) * band], axis=0, keepdims=True) for r in range(SOFTMAX_BANDS)]
    while len(parts) > 1:
        parts = [combine_fn(parts[2 * r], parts[2 * r + 1]) for r in range(len(parts) // 2)]
    return parts[0]


def _flash_t(i, tk, k_ref, vt_ref, qt, q_of_col, m_ref, acc_ref, sa_ref, sb_ref):
    m_ref[...] = jnp.full(m_ref.shape, NEG_INF, F32)
    acc_ref[...] = jnp.zeros(acc_ref.shape, F32)
    ones = jnp.ones((BF16_ROWS, tk), BF16)

    def scores(j, s_ref):
        start = pl.multiple_of(j * tk, tk)
        s_ref[...] = _dot(k_ref[pl.ds(start, tk), :], qt)

    def soft(j, s_ref, masked):
        start = pl.multiple_of(j * tk, tk)
        s = s_ref[...]
        if masked:
            key = lax.broadcasted_iota(jnp.int32, s.shape, 0)
            s = jnp.where(key <= q_of_col, s, NEG_INF)
        m_old = m_ref[...]
        m_new = jnp.maximum(m_old, _col_reduce(s, jnp.max, jnp.maximum))
        alpha = jnp.exp2(m_old - m_new)
        p = jnp.exp2(s - m_new)
        vt = jnp.concatenate([vt_ref[:, pl.ds(start, tk)], ones], axis=0)
        acc_ref[...] = alpha * acc_ref[...] + _dot(vt, p.astype(BF16))
        m_ref[...] = m_new

    scores(0, sa_ref)

    def body(jj, c):
        j = 2 * jj
        scores(j + 1, sb_ref)
        soft(j, sa_ref, False)
        scores(j + 2, sa_ref)
        soft(j + 1, sb_ref, False)
        return c

    lax.fori_loop(0, lax.shift_right_logical(i, 1), body, 0)
    odd = (i & 1) == 1

    @pl.when(odd)
    def _():
        scores(i, sb_ref)
        soft(i - 1, sa_ref, False)
        soft(i, sb_ref, True)

    @pl.when(jnp.logical_not(odd))
    def _():
        soft(i, sa_ref, True)


def _normalized(acc_ref):
    return acc_ref[0:HEAD_DIM, :] / acc_ref[HEAD_DIM:HEAD_DIM + 1, :]


def _diff_attn_kernel(q_ref, k_ref, vt_ref, lq1_ref, lk1_ref, lq2_ref, lk2_ref, gs_ref, o_ref,
                      m_ref, acc_ref, sa_ref, sb_ref, *, tq, lam_init):
    qt = q_ref[...].astype(F32).T
    sub = lax.broadcasted_iota(jnp.int32, qt.shape, 0)
    qt2 = jnp.concatenate([jnp.where(sub < DK_DIFF, qt, 0.0), jnp.where(sub < DK_DIFF, 0.0, qt)],
                          axis=1).astype(BF16)
    col = lax.broadcasted_iota(jnp.int32, (1, 2 * tq), 1) & (tq - 1)
    _flash_t(pl.program_id(2), tq, k_ref, vt_ref, qt2, col, m_ref, acc_ref, sa_ref, sb_ref)
    lam = _lambda(lq1_ref, lk1_ref, lq2_ref, lk2_ref, lam_init)
    o = _normalized(acc_ref)
    y = (o[:, :tq] - lam * o[:, tq:]).T
    y = y * lax.rsqrt(jnp.mean(y * y, axis=-1, keepdims=True) + EPS) * gs_ref[...]
    o_ref[...] = (y * (1.0 - lam_init)).astype(o_ref.dtype)


def _fox_attn_kernel(q_ref, k_ref, vt_ref, o_ref, m_ref, acc_ref, sa_ref, sb_ref, *, tq):
    qt = q_ref[...].astype(F32).T
    sub = lax.broadcasted_iota(jnp.int32, qt.shape, 0)
    ones = jnp.where(sub < BIAS_PARTS, 1.0, 0.0)
    qa = jnp.concatenate([qt, ones], axis=0).astype(BF16)
    col = lax.broadcasted_iota(jnp.int32, (1, tq), 1)
    _flash_t(pl.program_id(2), tq, k_ref, vt_ref, qa, col, m_ref, acc_ref, sa_ref, sb_ref)
    o_ref[...] = _normalized(acc_ref).T.astype(o_ref.dtype)


def _attn_call(kern, name, q, k, vt, extra, extra_specs, nb, t, tq, ncols):
    assert tq & (tq - 1) == 0 and tq % (SOFTMAX_BANDS * SUBLANES) == 0
    nt = t // tq
    kd = k.shape[-1]
    qspec = pl.BlockSpec((tq, HEAD_DIM), lambda b, h, i: (b * nt + i, h))
    score = pltpu.VMEM((tq, ncols), F32)
    return pl.pallas_call(
        kern, name=name,
        grid=(nb, N_HEADS, nt),
        in_specs=[qspec,
                  pl.BlockSpec((None, None, t, kd), lambda b, h, i: (b, h, 0, 0)),
                  pl.BlockSpec((None, None, HEAD_DIM, t), lambda b, h, i: (b, h, 0, 0))] + extra_specs,
        out_specs=qspec,
        out_shape=jax.ShapeDtypeStruct(q.shape, BF16),
        scratch_shapes=[pltpu.VMEM((1, ncols), F32), pltpu.VMEM((HEAD_DIM + BF16_ROWS, ncols), F32),
                        score, score],
        compiler_params=_cparams(3, 40 << 20),
    )(q, k, vt, *extra)


def _diff_attn(q, k, vt, lq1, lk1, lq2, lk2, g_sub, nb, t, tq, lam_init):
    vec = pl.BlockSpec((1, DK_DIFF), lambda b, h, i: (0, 0))
    return _attn_call(functools.partial(_diff_attn_kernel, tq=tq, lam_init=lam_init), "diff_attn", q, k, vt,
                      [lq1, lk1, lq2, lk2, g_sub],
                      [vec, vec, vec, vec, pl.BlockSpec((1, HEAD_DIM), lambda b, h, i: (0, 0))],
                      nb, t, tq, 2 * tq)


def _fox_attn(q, k, vt, nb, t, tq):
    return _attn_call(functools.partial(_fox_attn_kernel, tq=tq), "fox_attn", q, k, vt, [], [], nb, t, tq, tq)


def _head_rows(q, width, n_groups):
    qt = jnp.concatenate([q] * n_groups, axis=0)
    lane = lax.broadcasted_iota(jnp.int32, qt.shape, 1)
    grp = jnp.right_shift(lax.broadcasted_iota(jnp.int32, qt.shape, 0), 3)
    lo = grp * width
    return jnp.where((lane >= lo) & (lane < lo + width), qt, 0.0).astype(BF16)


def _expand_heads(x, n):
    return jnp.concatenate([jnp.broadcast_to(x[h:h + 1, :], (SUBLANES, n)) for h in range(N_HEADS)], axis=0)


def _flat_heads(ref):
    return jnp.concatenate([ref[h].astype(BF16) for h in range(N_HEADS)], axis=1)


def _dec_update(s, v_tiles, m_ref, l_ref, acc_ref):
    m_old = m_ref[...]
    m_new = jnp.maximum(m_old, jnp.max(s, axis=-1, keepdims=True))
    alpha = jnp.exp(m_old - m_new)
    p = jnp.exp(s - m_new)
    l_ref[...] = alpha * l_ref[...] + jnp.sum(p, axis=-1, keepdims=True)
    pb = p.astype(BF16)
    pv = None
    for r, v in enumerate(v_tiles):
        d = _dot(pb[:, r * LANES:(r + 1) * LANES], v)
        pv = d if pv is None else pv + d
    acc_ref[...] = alpha * acc_ref[...] + pv
    m_ref[...] = m_new


def _new_tokens(qr, kn_ref, vn_ref, bias_rows):
    def padded(ref):
        x = jnp.concatenate([ref[h] for h in range(N_HEADS)], axis=1)
        return jnp.concatenate([x, jnp.zeros((LANES - SUBLANES, ATT_W), F32)], axis=0).astype(BF16)

    s = _dot_nt(qr, padded(kn_ref))
    if bias_rows is not None:
        s = s + bias_rows
    keep = (lax.broadcasted_iota(jnp.int32, s.shape, 1)
            <= (lax.broadcasted_iota(jnp.int32, s.shape, 0) & (SUBLANES - 1)))
    return jnp.where(keep, s, NEG_INF), padded(vn_ref)


def _dec_init(q_ref, qr_ref, m_ref, l_ref, acc_ref, width, n_groups):
    qr_ref[...] = _head_rows(q_ref[...], width, n_groups)
    m_ref[...] = jnp.full_like(m_ref, NEG_INF)
    l_ref[...] = jnp.zeros_like(l_ref)
    acc_ref[...] = jnp.zeros_like(acc_ref)


def _diff_dec_kernel(pt_ref, q_ref, kn_ref, vn_ref, lq1_ref, lk1_ref, lq2_ref, lk2_ref, gs_ref, *rest,
                     npg, lam_init):
    k_refs = rest[:npg]
    v_refs = rest[npg:2 * npg]
    o_ref, qr_ref, m_ref, l_ref, acc_ref = rest[2 * npg:]
    j = pl.program_id(1)

    @pl.when(j == 0)
    def _():
        _dec_init(q_ref, qr_ref, m_ref, l_ref, acc_ref, DK_DIFF, 2 * N_HEADS)

    qr = qr_ref[...]
    s = jnp.concatenate([_dot_nt(qr, _flat_heads(k)) for k in k_refs], axis=1)
    _dec_update(s, [_flat_heads(v) for v in v_refs], m_ref, l_ref, acc_ref)

    @pl.when(j == pl.num_programs(1) - 1)
    def _():
        sn, vn = _new_tokens(qr, kn_ref, vn_ref, None)
        _dec_update(sn, [vn], m_ref, l_ref, acc_ref)
        lam = _lambda(lq1_ref, lk1_ref, lq2_ref, lk2_ref, lam_init)
        acc = acc_ref[...]
        l = l_ref[...]
        for h in range(N_HEADS):
            r1 = slice(2 * h * SUBLANES, (2 * h + 1) * SUBLANES)
            r2 = slice((2 * h + 1) * SUBLANES, (2 * h + 2) * SUBLANES)
            cl = slice(h * HEAD_DIM, (h + 1) * HEAD_DIM)
            o_ref[:, cl] = _diff_combine(acc[r1, cl], l[r1], acc[r2, cl], l[r2], lam, gs_ref[...], lam_init)


def _fox_dec_kernel(pt_ref, q_ref, kn_ref, vn_ref, bn_ref, tri_ref, *rest, npg):
    k_refs = rest[:npg]
    v_refs = rest[npg:2 * npg]
    lf_refs = rest[2 * npg:3 * npg]
    o_ref, qr_ref, m_ref, l_ref, acc_ref, carry_ref = rest[3 * npg:]
    j = pl.program_id(1)

    @pl.when(j == 0)
    def _():
        _dec_init(q_ref, qr_ref, m_ref, l_ref, acc_ref, HEAD_DIM, N_HEADS)
        carry_ref[...] = jnp.zeros_like(carry_ref)

    terms = []
    for lf in lf_refs:
        terms += list(_split3(lf[...]))
    y = _dot(jnp.concatenate(terms, axis=0).astype(BF16), tri_ref[...])
    c = carry_ref[...]
    biases = []
    rows = BIAS_PARTS * SUBLANES
    for r in range(npg):
        yr = y[r * rows:(r + 1) * rows]
        blk = (yr[0:SUBLANES] + yr[SUBLANES:2 * SUBLANES]) + yr[2 * SUBLANES:3 * SUBLANES] + c
        biases.append(-blk)
        c = jnp.broadcast_to(blk[:, LANES - 1:LANES], blk.shape)
    carry_ref[...] = c

    qr = qr_ref[...]
    s = jnp.concatenate([_dot_nt(qr, _flat_heads(k)) for k in k_refs], axis=1)
    s = s + _expand_heads(jnp.concatenate(biases, axis=1), npg * LANES)
    _dec_update(s, [_flat_heads(v) for v in v_refs], m_ref, l_ref, acc_ref)

    @pl.when(j == pl.num_programs(1) - 1)
    def _():
        sn, vn = _new_tokens(qr, kn_ref, vn_ref, _expand_heads(bn_ref[...] - c, LANES))
        _dec_update(sn, [vn], m_ref, l_ref, acc_ref)
        acc = acc_ref[...]
        l = l_ref[...]
        for h in range(N_HEADS):
            rs = slice(h * SUBLANES, (h + 1) * SUBLANES)
            cl = slice(h * HEAD_DIM, (h + 1) * HEAD_DIM)
            o_ref[:, cl] = acc[rs, cl] / l[rs]


def _page_specs(layer, n_pages, npg, block):
    nz = (0,) * (len(block) - 2)

    def spec(r):
        return pl.BlockSpec(block, lambda b, j, pt, r=r: (layer, pt[b * n_pages + j * npg + r]) + nz)
    return [spec(r) for r in range(npg)]


def _dec_common(layer, tnew, rows):
    tok = pl.BlockSpec((tnew, ATT_W), lambda b, j, pt: (b, 0))
    new_kv = pl.BlockSpec((None, None, N_HEADS, tnew, HEAD_DIM), lambda b, j, pt: (layer, b, 0, 0, 0))
    scratch = [pltpu.VMEM((rows, ATT_W), BF16), pltpu.VMEM((rows, 1), F32), pltpu.VMEM((rows, 1), F32),
               pltpu.VMEM((rows, ATT_W), F32)]
    return tok, new_kv, scratch


def _diff_decode(pt, q, kn, vn, lq1, lk1, lq2, lk2, g_sub, cache_k, cache_v, layer, nb, n_pages, lam_init):
    tnew = q.shape[0] // nb
    assert tnew == SUBLANES
    page = cache_k.shape[3]
    npg = _tile(n_pages, DEC_PAGES_PER_STEP)
    tok, new_kv, scratch = _dec_common(layer, tnew, 2 * N_HEADS * SUBLANES)
    vec = pl.BlockSpec((1, DK_DIFF), lambda b, j, pt: (0, 0))
    pages = _page_specs(layer, n_pages, npg, (None, None, N_HEADS, page, HEAD_DIM))
    return pl.pallas_call(
        functools.partial(_diff_dec_kernel, npg=npg, lam_init=lam_init), name="diff_decode",
        grid_spec=pltpu.PrefetchScalarGridSpec(
            num_scalar_prefetch=1, grid=(nb, n_pages // npg),
            in_specs=[tok, new_kv, new_kv, vec, vec, vec, vec,
                      pl.BlockSpec((1, HEAD_DIM), lambda b, j, pt: (0, 0))] + pages + pages,
            out_specs=tok, scratch_shapes=scratch),
        out_shape=jax.ShapeDtypeStruct(q.shape, F32),
        compiler_params=_cparams(2, 4 * npg * math.prod(cache_k.shape[2:]) * 4 + (20 << 20)),
    )(pt, q, kn, vn, lq1, lk1, lq2, lk2, g_sub, *([cache_k] * npg), *([cache_v] * npg))


def _fox_decode(pt, q, kn, vn, bias_new, cache_k, cache_v, cache_lf, layer, nb, n_pages):
    tnew = q.shape[0] // nb
    assert tnew == SUBLANES
    page = cache_k.shape[3]
    assert page == LANES
    npg = _tile(n_pages, DEC_PAGES_PER_STEP)
    tok, new_kv, scratch = _dec_common(layer, tnew, N_HEADS * SUBLANES)
    pages = _page_specs(layer, n_pages, npg, (None, None, N_HEADS, page, HEAD_DIM))
    lf_pages = _page_specs(layer, n_pages, npg, (None, None, SUBLANES, page))
    idx = jnp.arange(page, dtype=jnp.int32)
    tri = (idx[:, None] <= idx[None, :]).astype(BF16)
    return pl.pallas_call(
        functools.partial(_fox_dec_kernel, npg=npg), name="fox_decode",
        grid_spec=pltpu.PrefetchScalarGridSpec(
            num_scalar_prefetch=1, grid=(nb, n_pages // npg),
            in_specs=[tok, new_kv, new_kv,
                      pl.BlockSpec((None, SUBLANES, LANES), lambda b, j, pt: (b, 0, 0)),
                      pl.BlockSpec((page, page), lambda b, j, pt: (0, 0))]
            + pages + pages + lf_pages,
            out_specs=tok, scratch_shapes=scratch + [pltpu.VMEM((SUBLANES, LANES), F32)]),
        out_shape=jax.ShapeDtypeStruct(q.shape, F32),
        compiler_params=_cparams(2, 4 * npg * math.prod(cache_k.shape[2:]) * 4 + (20 << 20)),
    )(pt, q, kn, vn, bias_new, tri, *([cache_k] * npg), *([cache_v] * npg), *([cache_lf] * npg))


def _out_proj_kernel(yc_ref, yd_ref, yf_ref, w_ref, x_ref, g1_ref, gn_ref, sc_ref, sh_ref, x1_ref, h2_ref):
    mix = jnp.concatenate([yc_ref[...].astype(BF16), yd_ref[...].astype(BF16), yf_ref[...].astype(BF16)], axis=1)
    x1 = x_ref[...] + g1_ref[...] * _dot(mix, w_ref[...])
    x1_ref[...] = x1
    h2_ref[...] = _norm_mod_value(x1, gn_ref[...], sc_ref[...], sh_ref[...]).astype(h2_ref.dtype)


def _out_proj(yc, yd, yf, w_out, layer, x, g1, gn, sc2, sh2, tm, tiles_per_group):
    m, d = x.shape
    row = lambda i: (i, 0)
    return pl.pallas_call(
        _out_proj_kernel, name="out_proj",
        grid=(m // tm,),
        in_specs=[pl.BlockSpec((tm, C_CONV), row), pl.BlockSpec((tm, ATT_W), row), pl.BlockSpec((tm, ATT_W), row),
                  _layer_spec(w_out, layer), pl.BlockSpec((tm, d), row), _mod_spec(g1, tiles_per_group),
                  _full_spec((1, d)), _mod_spec(sc2, tiles_per_group), _mod_spec(sh2, tiles_per_group)],
        out_specs=[pl.BlockSpec((tm, d), row), pl.BlockSpec((tm, d), row)],
        out_shape=[jax.ShapeDtypeStruct((m, d), F32), jax.ShapeDtypeStruct((m, d), BF16)],
        compiler_params=_cparams(1, 40 << 20),
    )(yc, yd, yf, w_out, x, g1, gn, sc2, sh2)


def _ffn_up_kernel(h_ref, wa_ref, wb_ref, g_ref):
    hb = h_ref[...]
    a = _dot(hb, wa_ref[...].astype(BF16))
    b = _dot(hb, wb_ref[...].astype(BF16))
    g_ref[...] = (a * jax.nn.sigmoid(a) * b).astype(g_ref.dtype)


def _ffn_up(h2, w_in, layer, tm, tf):
    m, d = h2.shape
    f = w_in.shape[2] // 2
    nf = f // tf
    return pl.pallas_call(
        _ffn_up_kernel, name="ffn_up",
        grid=(m // tm, nf),
        in_specs=[pl.BlockSpec((tm, d), lambda i, j: (i, 0)),
                  pl.BlockSpec((None, d, tf), lambda i, j: (layer, 0, j)),
                  pl.BlockSpec((None, d, tf), lambda i, j: (layer, 0, j + nf))],
        out_specs=pl.BlockSpec((tm, tf), lambda i, j: (i, j)),
        out_shape=jax.ShapeDtypeStruct((m, f), BF16),
        compiler_params=_cparams(2, 40 << 20),
    )(h2, w_in, w_in)


def _ffn_down_kernel(g_ref, w_ref, x_ref, g2_ref, *rest, next_norm):
    x2 = x_ref[...] + g2_ref[...] * _dot(g_ref[...], w_ref[...])
    if next_norm:
        gn_ref, sc_ref, sh_ref, x2_ref, hn_ref = rest
        hn_ref[...] = _norm_mod_value(x2, gn_ref[...], sc_ref[...], sh_ref[...]).astype(hn_ref.dtype)
    else:
        x2_ref, = rest
    x2_ref[...] = x2


def _ffn_down(g, w_out, layer, x1, g2, nxt, tm, tiles_per_group):
    m, f = g.shape
    d = w_out.shape[2]
    xspec = pl.BlockSpec((tm, d), lambda i: (i, 0))
    in_specs = [pl.BlockSpec((tm, f), lambda i: (i, 0)),
                pl.BlockSpec((None, f, d), lambda i: (layer, 0, 0), pipeline_mode=pl.Buffered(1)),
                xspec, _mod_spec(g2, tiles_per_group)]
    args = [g, w_out, x1, g2]
    out_specs, out_shape = [xspec], [jax.ShapeDtypeStruct((m, d), F32)]
    if nxt is not None:
        in_specs += [_full_spec((1, d)), _mod_spec(nxt[1], tiles_per_group), _mod_spec(nxt[2], tiles_per_group)]
        args += list(nxt)
        out_specs.append(xspec)
        out_shape.append(jax.ShapeDtypeStruct((m, d), BF16))
    out = pl.pallas_call(
        functools.partial(_ffn_down_kernel, next_norm=nxt is not None), name="ffn_down",
        grid=(m // tm,),
        in_specs=in_specs, out_specs=out_specs, out_shape=out_shape,
        compiler_params=_cparams(1, f * d * 2 + 6 * tm * (f + 4 * d) + (8 << 20)),
    )(*args)
    return (out[0], out[1]) if nxt is not None else (out[0], None)


def _rope_tables(pos):
    half = DK_DIFF // 2
    inv = ROPE_THETA ** (-jnp.arange(half, dtype=F32) / half)
    ang = pos.astype(F32)[:, None] * inv[None, :]
    cos = jnp.tile(jnp.cos(ang), (1, LANES // half))
    sin = jnp.sin(ang)
    sin = jnp.tile(jnp.concatenate([-sin, sin], axis=1), (1, LANES // DK_DIFF))
    return cos, sin


class _Group:
    def __init__(self, nb, t, is_prompt):
        self.nb, self.t, self.is_prompt = nb, t, is_prompt
        self.m = nb * t
        if is_prompt:
            self.tm = _tile(t, 256)
            self.tm_out = _tile(t, 512)
            self.tm_conv = _tile(t, 512)
            self.tm_ffn = _tile(t, 1024)
            self.tm_down = _tile(t, 256)
            self.tq = _tile(t, 512)
            self.down_tiles = t // self.tm_down
            self.out_tiles = t // self.tm_out
        else:
            self.tm = self.tm_out = self.tm_conv = self.tm_ffn = self.tm_down = self.m
            self.down_tiles = self.out_tiles = 1


def _mods(ada, grp):
    d = ada.shape[1] // 6
    parts = jnp.split(ada, 6, axis=-1)
    if grp.is_prompt:
        return [p.reshape(grp.nb, 1, d) for p in parts]
    return [jnp.repeat(p, grp.t, axis=0).reshape(1, grp.m, d) for p in parts]


def _layer(x, h, grp, wts, lw, layer, depth, mods, nxt, rope, lam_init, states, ctx):
    sh1, sc1, g1, sh2, sc2, g2 = mods
    cos, sin = rope
    nb, t = grp.nb, grp.t
    prompt = grp.is_prompt
    prev_d = None if states is None else states[0:2]
    prev_f = None if states is None else states[2:4]

    yc, conv_new = _proj_conv(h, wts["w_cv"], wts["w_cg"], layer, lw["conv_w"], lw["conv_b"], lw["conv_ln_g"],
                              lw["conv_ln_b"], None if prompt else ctx["state_pad"],
                              nb, t, grp.tm_conv, BF16 if prompt else F32)
    dres = _proj_diff(h, wts["w_dq"], wts["w_dk"], wts["w_dv"], layer, depth, lw["diff_qn"], lw["diff_kn"],
                      cos, sin, prev_d, nb, t, grp.tm, prompt)
    fres = _proj_fox(h, wts["w_fq"], wts["w_fk"], wts["w_fv"], wts["w_ff"], layer, depth, lw["fox_fb"],
                     lw["fox_qn"], lw["fox_kn"], prev_f, nb, t, grp.tm, prompt)
    qd, kd, vd = dres[:3]
    qf, kf, vf, lft, fbias = fres[:5]
    if prompt:
        yd = _diff_attn(qd, dres[3], dres[4], lw["lam_q1"], lw["lam_k1"], lw["lam_q2"], lw["lam_k2"],
                        lw["diff_subln"], nb, t, grp.tq, lam_init)
        yf = _fox_attn(qf, fres[5], fres[6], nb, t, grp.tq)
    else:
        yd = _diff_decode(ctx["pt"], qd, kd, vd, lw["lam_q1"], lw["lam_k1"], lw["lam_q2"], lw["lam_k2"],
                          lw["diff_subln"], ctx["cache_dk"], ctx["cache_dv"], layer, nb, ctx["n_pages"], lam_init)
        bn = jnp.transpose(fbias[0, :, :nb * t].reshape(SUBLANES, nb, t), (1, 0, 2))
        bn = jnp.pad(bn, ((0, 0), (0, 0), (0, LANES - t)))
        yf = _fox_decode(ctx["pt"], qf, kf, vf, bn, ctx["cache_fk"], ctx["cache_fv"], ctx["cache_lf"],
                         layer, nb, ctx["n_pages"])
    x1, h2 = _out_proj(yc, yd, yf, wts["w_out"], layer, x, g1, lw["norm_ffn"], sc2, sh2, grp.tm_out,
                       grp.out_tiles)
    g = _ffn_up(h2, wts["w_ffn_in"], layer, grp.tm_ffn, wts["tf"])
    x2, h_next = _ffn_down(g, wts["w_ffn_out"], layer, x1, g2, nxt, grp.tm_down, grp.down_tiles)
    return x2, h_next, (kd, vd, kf, vf), lft, conv_new


def kernel(x_prompt, x_sample, cache_diff_k, cache_diff_v, cache_fox_k, cache_fox_v, cache_fox_lf, state_conv,
           page_table, c_prompt, c_sample, norm_mix, norm_ffn, w_ada, b_ada, w_in, conv_w, conv_b, conv_ln_g,
           conv_ln_b, diff_qn, diff_kn, lam_q1, lam_k1, lam_q2, lam_k2, diff_subln, fox_qn, fox_kn, fox_fb,
           w_out, w_ffn_in, w_ffn_out):
    depth = w_in.shape[0]
    bp, tp, d = x_prompt.shape
    bs, ts, _ = x_sample.shape
    page = cache_diff_k.shape[2]
    n_pages = page_table.shape[1]
    d_ff = w_ffn_out.shape[1]
    assert d_ff % LANES == 0
    gp = _Group(bp, tp, True)
    gs = _Group(bs, ts, False)

    c_all = jnp.concatenate([c_prompt, c_sample], axis=0)
    n_c = c_all.shape[0]
    c_all = jnp.pad(c_all, ((0, (-n_c) % (2 * SUBLANES)), (0, 0)))
    ada = _ada(c_all, w_ada, b_ada)
    mods_p = [_mods(ada[l, :bp], gp) for l in range(depth)]
    mods_s = [_mods(ada[l, bp:bp + bs], gs) for l in range(depth)]

    cuts = [0, C_CONV, 2 * C_CONV]
    for _ in range(6):
        cuts.append(cuts[-1] + ATT_W)
    names = ["w_cv", "w_cg", "w_dq", "w_dk", "w_dv", "w_fq", "w_fk", "w_fv"]
    wts = {n: w_in[:, :, cuts[i]:cuts[i + 1]].astype(BF16) for i, n in enumerate(names)}
    wts["w_ff"] = jnp.pad(w_in[:, :, cuts[-1]:], ((0, 0), (0, 0), (0, LANES - N_HEADS))).astype(BF16)
    wts["w_out"] = w_out.astype(BF16)
    wts["w_ffn_in"] = w_ffn_in
    wts["w_ffn_out"] = w_ffn_out.astype(BF16)
    wts["tf"] = _tile(d_ff, 512)

    rope_p = _rope_tables(jnp.arange(tp, dtype=jnp.int32))
    cos_s, sin_s = _rope_tables(n_pages * page + jnp.arange(ts, dtype=jnp.int32))
    rope_s = (jnp.tile(cos_s, (bs, 1)), jnp.tile(sin_s, (bs, 1)))

    head_major = lambda c: jnp.transpose(c, (0, 1, 3, 2, 4))
    ctx = dict(pt=page_table.reshape(-1).astype(jnp.int32), n_pages=n_pages,
               cache_dk=head_major(cache_diff_k), cache_dv=head_major(cache_diff_v),
               cache_fk=head_major(cache_fox_k), cache_fv=head_major(cache_fox_v),
               cache_lf=jnp.pad(jnp.transpose(cache_fox_lf, (0, 1, 3, 2)),
                                ((0, 0), (0, 0), (0, SUBLANES - N_HEADS), (0, 0))))
    state_pad = jnp.pad(state_conv, ((0, 0), (0, 0), (HIST_PAD - CONV_HIST, 0), (0, 0)))

    xp = x_prompt.reshape(bp * tp, d)
    xs = x_sample.reshape(bs * ts, d)
    g0 = norm_mix[0][None, :]
    hp = _norm_mod(xp, g0, mods_p[0][1], mods_p[0][0], gp.tm_down, gp.down_tiles)
    hs = _norm_mod(xs, g0, mods_s[0][1], mods_s[0][0], gs.tm, 1)
    kv_p = kv_s = None
    lf_p, lf_s, conv_p, conv_s = [], [], [], []
    for l in range(depth):
        lam_init = 0.8 - 0.6 * math.exp(-0.3 * l)
        lw = dict(
            conv_w=conv_w[l], conv_b=conv_b[l][None, :], conv_ln_g=conv_ln_g[l][None, :],
            conv_ln_b=conv_ln_b[l][None, :],
            diff_qn=jnp.tile(diff_qn[l], LANES // DK_DIFF)[None, :],
            diff_kn=jnp.tile(diff_kn[l], LANES // DK_DIFF)[None, :],
            lam_q1=lam_q1[l][None, :], lam_k1=lam_k1[l][None, :], lam_q2=lam_q2[l][None, :],
            lam_k2=lam_k2[l][None, :], diff_subln=diff_subln[l][None, :],
            fox_qn=fox_qn[l][None, :], fox_kn=fox_kn[l][None, :],
            fox_fb=jnp.pad(fox_fb[l], (0, LANES - N_HEADS))[None, :],
            norm_ffn=norm_ffn[l][None, :])
        last = l == depth - 1
        nxt_p = None if last else (norm_mix[l + 1][None, :], mods_p[l + 1][1], mods_p[l + 1][0])
        nxt_s = None if last else (norm_mix[l + 1][None, :], mods_s[l + 1][1], mods_s[l + 1][0])
        xp, hp, kv_p, lft, cn = _layer(xp, hp, gp, wts, lw, l, depth, mods_p[l], nxt_p, rope_p, lam_init, kv_p,
                                       None)
        lf_p.append(lft)
        conv_p.append(cn)
        xs, hs, kv_s, lft, cn = _layer(xs, hs, gs, wts, lw, l, depth, mods_s[l], nxt_s, rope_s, lam_init, kv_s,
                                       dict(ctx, state_pad=state_pad[l]))
        lf_s.append(lft)
        conv_s.append(cn)

    kv = lambda a: jnp.transpose(a, (0, 1, 3, 2, 4))
    lf_p = jnp.transpose(jnp.stack(lf_p, axis=0)[:, :, :N_HEADS, :], (0, 1, 3, 2))
    lf_s = jnp.stack(lf_s, axis=0)[:, 0, :N_HEADS, :bs * ts]
    lf_s = jnp.transpose(lf_s.reshape(depth, N_HEADS, bs, ts), (0, 2, 3, 1))
    return (xp.reshape(bp, tp, d), xs.reshape(bs, ts, d),
            kv(kv_p[0]), kv(kv_p[1]), kv(kv_p[2]), kv(kv_p[3]), lf_p, jnp.stack(conv_p, axis=0),
            kv(kv_s[0]), kv(kv_s[1]), kv(kv_s[2]), kv(kv_s[3]), lf_s, jnp.stack(conv_s, axis=0))
```

```python
import functools
import math

import jax
import jax.numpy as jnp
from jax import lax
from jax.experimental import pallas as pl
from jax.experimental.pallas import tpu as pltpu

F32 = jnp.float32
BF16 = jnp.bfloat16

EPS = 1e-6
NEG_INF = -1e30
ROPE_THETA = 10000.0
C_CONV = 512
CONV_WIDTH = 31
CONV_HIST = CONV_WIDTH - 1
N_HEADS = 6
HEAD_DIM = 128
DK_DIFF = 64
ATT_W = N_HEADS * HEAD_DIM
N_PAIRS = N_HEADS // 2
PAIR_W = 2 * HEAD_DIM
N_W = 1 + 3 * N_PAIRS
DIFF_COL0 = 2 * C_CONV
FOX_COL0 = DIFF_COL0 + 3 * ATT_W

LANES = 128
SUBLANES = 8
V7X_VMEM_BYTES = 64 * 1024 * 1024
VMEM_CAP = V7X_VMEM_BYTES - 8 * 1024 * 1024
HIST_PAD = 32
CONV_CHUNK = 64
BIAS_PARTS = 3
LOG2E = 1.4426950408889634
BF16_ROWS = 16

DEC_PAGES_PER_STEP = 16


def _dot(a, b):
    return jnp.dot(a, b, preferred_element_type=F32)


def _dot_nt(a, b):
    return lax.dot_general(a, b, (((1,), (1,)), ((), ())), preferred_element_type=F32)


def _cparams(n_grid, vmem_bytes):
    return pltpu.CompilerParams(dimension_semantics=("arbitrary",) * n_grid,
                                vmem_limit_bytes=int(min(max(vmem_bytes, 16 << 20), VMEM_CAP)))


def _tile(n, pref):
    t = min(n, pref)
    assert n % t == 0, (n, pref)
    return t


def _full_spec(shape):
    nd = len(shape)
    return pl.BlockSpec(shape, lambda *_: (0,) * nd)


def _layer_spec(w, layer):
    return pl.BlockSpec((None,) + tuple(w.shape[1:]), lambda *_: (layer, 0, 0), pipeline_mode=pl.Buffered(1))


def _col_spec(w, layer, width, block):
    return pl.BlockSpec((None, w.shape[1], width), lambda *_: (layer, 0, block), pipeline_mode=pl.Buffered(1))


def _mod_spec(mod, tiles_per_group):
    _, r, d = mod.shape
    return pl.BlockSpec((None, r, d), lambda i, *_: (i // tiles_per_group, 0, 0))


def _split3(x):
    hi = x.astype(BF16).astype(F32)
    r1 = x - hi
    mid = r1.astype(BF16).astype(F32)
    lo = (r1 - mid).astype(BF16).astype(F32)
    return hi, mid, lo


def _ada_kernel(c_ref, w_ref, b_ref, o_ref):
    c = c_ref[...]
    a = c * jax.nn.sigmoid(c)
    a_hi = a.astype(BF16)
    a_lo = (a - a_hi.astype(F32)).astype(BF16)
    w = w_ref[...].astype(BF16)
    o_ref[...] = _dot(a_hi, w) + _dot(a_lo, w) + b_ref[...]


def _ada(c_all, w_ada, b_ada):
    depth, d, n = w_ada.shape
    rows = c_all.shape[0]
    tn = _tile(n, 512)
    return pl.pallas_call(
        _ada_kernel, name="ada",
        grid=(depth, n // tn),
        in_specs=[pl.BlockSpec((rows, d), lambda l, j: (0, 0)),
                  pl.BlockSpec((None, d, tn), lambda l, j: (l, 0, j)),
                  pl.BlockSpec((None, 1, tn), lambda l, j: (l, 0, j))],
        out_specs=pl.BlockSpec((None, rows, tn), lambda l, j: (l, 0, j)),
        out_shape=jax.ShapeDtypeStruct((depth, rows, n), F32),
        compiler_params=_cparams(2, 6 * d * tn * 4),
    )(c_all, w_ada, b_ada.reshape(depth, 1, n))


def _norm_mod_value(x, g, sc, sh):
    y = x * lax.rsqrt(jnp.mean(x * x, axis=-1, keepdims=True) + EPS) * g
    return y * (1.0 + sc) + sh


def _norm_mod_kernel(x_ref, g_ref, sc_ref, sh_ref, h_ref):
    h_ref[...] = _norm_mod_value(x_ref[...], g_ref[...], sc_ref[...], sh_ref[...]).astype(h_ref.dtype)


def _norm_mod(x, g, sc, sh, tm, tiles_per_group):
    m, d = x.shape
    return pl.pallas_call(
        _norm_mod_kernel, name="norm_mod",
        grid=(m // tm,),
        in_specs=[pl.BlockSpec((tm, d), lambda i: (i, 0)), _full_spec((1, d)),
                  _mod_spec(sc, tiles_per_group), _mod_spec(sh, tiles_per_group)],
        out_specs=pl.BlockSpec((tm, d), lambda i: (i, 0)),
        out_shape=jax.ShapeDtypeStruct((m, d), BF16),
        compiler_params=_cparams(1, 8 * tm * d * 4),
    )(x, g, sc, sh)


def _conv_rows(ext_ref, z_ref, cw_ref, cb_ref, lg_ref, lb_ref, r0, n):
    acc = jnp.broadcast_to(cb_ref[...], (n, C_CONV))
    off = HIST_PAD - CONV_HIST
    for r in range(SUBLANES):
        span = n if r == 0 else n + SUBLANES
        z = None
        for a in range((CONV_WIDTH + off) // SUBLANES + 1):
            j = SUBLANES * a + r - off
            if 0 <= j < CONV_WIDTH:
                term = cw_ref[j:j + 1, :] * ext_ref[r0 + SUBLANES * a:r0 + SUBLANES * a + span, :]
                z = term if z is None else z + term
        if r == 0:
            acc = acc + z
        else:
            z_ref[r] = z
            acc = acc + z_ref[r, r:r + n, :]
    mu = jnp.mean(acc, axis=-1, keepdims=True)
    xc = acc - mu
    var = jnp.mean(xc * xc, axis=-1, keepdims=True)
    y = xc * lax.rsqrt(var + EPS) * lg_ref[...] + lb_ref[...]
    return y * jax.nn.sigmoid(y)


def _proj_conv_kernel(*refs, nseq, rows, carry):
    if carry:
        h_ref, wv_ref, wg_ref, cw_ref, cb_ref, lg_ref, lb_ref, yc_ref, cn_ref, ext_ref, z_ref = refs
        st_ref = None
    else:
        h_ref, wv_ref, wg_ref, cw_ref, cb_ref, lg_ref, lb_ref, st_ref, yc_ref, cn_ref, ext_ref, z_ref = refs
    hb = h_ref[...]
    u = _dot(hb, wv_ref[...]) * jax.nn.sigmoid(_dot(hb, wg_ref[...]))
    chunk = min(rows, CONV_CHUNK)
    for s in range(nseq):
        if carry:
            @pl.when(pl.program_id(1) == 0)
            def _():
                ext_ref[0:HIST_PAD, :] = jnp.zeros((HIST_PAD, C_CONV), F32)
        else:
            ext_ref[0:HIST_PAD, :] = st_ref[s]
        ext_ref[HIST_PAD:HIST_PAD + rows, :] = u[s * rows:(s + 1) * rows, :]
        for r0 in range(0, rows, chunk):
            y = _conv_rows(ext_ref, z_ref, cw_ref, cb_ref, lg_ref, lb_ref, r0, chunk)
            yc_ref[s * rows + r0:s * rows + r0 + chunk, :] = y.astype(yc_ref.dtype)
        new_hist = ext_ref[rows + HIST_PAD - CONV_HIST:rows + HIST_PAD, :]
        if carry:
            tail = ext_ref[rows:rows + HIST_PAD, :]
            ext_ref[0:HIST_PAD, :] = tail

            @pl.when(pl.program_id(1) == pl.num_programs(1) - 1)
            def _():
                cn_ref[0] = new_hist
        else:
            cn_ref[s] = new_hist


def _proj_conv(h, w_in, layer, conv_w, conv_b, ln_g, ln_b, state_pad, nb, t, tm, y_dtype):
    m, d = h.shape
    carry = state_pad is None
    small = [_col_spec(w_in, layer, C_CONV, 0), _col_spec(w_in, layer, C_CONV, 1),
             _full_spec((CONV_WIDTH, C_CONV)),
             _full_spec((1, C_CONV)), _full_spec((1, C_CONV)), _full_spec((1, C_CONV))]
    if carry:
        assert tm >= HIST_PAD
        nt = t // tm
        grid = (nb, nt)
        in_specs = [pl.BlockSpec((tm, d), lambda b, i: (b * nt + i, 0))] + small
        out_specs = [pl.BlockSpec((tm, C_CONV), lambda b, i: (b * nt + i, 0)),
                     pl.BlockSpec((1, CONV_HIST, C_CONV), lambda b, i: (b, 0, 0))]
        kern = functools.partial(_proj_conv_kernel, nseq=1, rows=tm, carry=True)
        args = (h, w_in, w_in, conv_w, conv_b, ln_g, ln_b)
        rows = tm
    else:
        grid = (1, 1)
        in_specs = [_full_spec((m, d))] + small + [_full_spec((nb, HIST_PAD, C_CONV))]
        out_specs = [_full_spec((m, C_CONV)), _full_spec((nb, CONV_HIST, C_CONV))]
        kern = functools.partial(_proj_conv_kernel, nseq=nb, rows=t, carry=False)
        args = (h, w_in, w_in, conv_w, conv_b, ln_g, ln_b, state_pad)
        rows = t
    return pl.pallas_call(
        kern, name="proj_conv", grid=grid, in_specs=in_specs, out_specs=out_specs,
        out_shape=[jax.ShapeDtypeStruct((m, C_CONV), y_dtype),
                   jax.ShapeDtypeStruct((nb, CONV_HIST, C_CONV), F32)],
        scratch_shapes=[pltpu.VMEM((HIST_PAD + rows, C_CONV), F32),
                        pltpu.VMEM((SUBLANES, min(rows, CONV_CHUNK) + SUBLANES, C_CONV), F32)],
        compiler_params=_cparams(2, 24 << 20),
    )(*args)


def _store_heads(ref, hh, x, nseq, rows):
    for s in range(nseq):
        ref[s, hh] = x[s * rows:(s + 1) * rows, :].astype(ref.dtype)


def _halfnorm_rope(x, g, cos, sin, lane):
    sq = x * x
    lo = lane < DK_DIFF
    s_lo = jnp.sum(jnp.where(lo, sq, 0.0), axis=-1, keepdims=True)
    s_hi = jnp.sum(jnp.where(lo, 0.0, sq), axis=-1, keepdims=True)
    ms = jnp.where(lo, s_lo, s_hi) * (1.0 / DK_DIFF)
    y = x * lax.rsqrt(ms + EPS) * g
    half = DK_DIFF // 2
    rot = jnp.where((lane & (DK_DIFF - 1)) < half,
                    pltpu.roll(y, LANES - half, 1), pltpu.roll(y, half, 1))
    return y * cos + rot * sin


def _proj_diff_kernel(*refs, nseq, rows, n_alias, attn_ops):
    h_ref = refs[0]
    wq_refs, wk_refs, wv_refs = refs[1:1 + N_PAIRS], refs[1 + N_PAIRS:1 + 2 * N_PAIRS], refs[1 + 2 * N_PAIRS:N_W]
    gq_ref, gk_ref, cos_ref, sin_ref = refs[N_W:N_W + 4]
    outs = refs[N_W + 4 + n_alias:]
    q_ref, ks_ref, vs_ref = outs[:3]
    hb = h_ref[...]
    tm = hb.shape[0]
    cos = cos_ref[...]
    sin = sin_ref[...]
    lane = lax.broadcasted_iota(jnp.int32, (tm, LANES), 1)
    scale = DK_DIFF ** -0.5 * (LOG2E if attn_ops else 1.0)
    for pair in range(N_PAIRS):
        zq = _dot(hb, wq_refs[pair][...])
        zk = _dot(hb, wk_refs[pair][...])
        zv = _dot(hb, wv_refs[pair][...])
        for sub in range(2):
            hh = 2 * pair + sub
            sl = slice(sub * HEAD_DIM, (sub + 1) * HEAD_DIM)
            q = _halfnorm_rope(zq[:, sl], gq_ref[...], cos, sin, lane) * scale
            q_ref[:, hh * HEAD_DIM:(hh + 1) * HEAD_DIM] = q.astype(q_ref.dtype)
            k = _halfnorm_rope(zk[:, sl], gk_ref[...], cos, sin, lane)
            v = zv[:, sl]
            _store_heads(ks_ref, hh, k, nseq, rows)
            _store_heads(vs_ref, hh, v, nseq, rows)
            if attn_ops:
                kb_ref, vt_ref = outs[3:5]
                kb_ref[0, hh] = k.astype(BF16)
                vt_ref[0, hh] = v.T.astype(BF16)


def _state_out(stack_shape, layer, nseq, rows):
    spec = pl.BlockSpec((None, nseq, N_HEADS, rows, HEAD_DIM), lambda b, i: (layer, b, 0, i, 0))
    return spec, jax.ShapeDtypeStruct(stack_shape, F32)


def _attn_operand_out(nb, t, tm, kd):
    specs = [pl.BlockSpec((1, N_HEADS, tm, kd), lambda b, i: (b, 0, i, 0)),
             pl.BlockSpec((1, N_HEADS, HEAD_DIM, tm), lambda b, i: (b, 0, 0, i))]
    shapes = [jax.ShapeDtypeStruct((nb, N_HEADS, t, kd), BF16),
              jax.ShapeDtypeStruct((nb, N_HEADS, HEAD_DIM, t), BF16)]
    return specs, shapes


def _alias_args(prev, n_in, first_out):
    specs = [pl.BlockSpec(memory_space=pl.ANY)] * len(prev)
    return list(prev), specs, {n_in + k: first_out + k for k in range(len(prev))}


def _qkv_specs(w_in, layer, first_col):
    base = first_col // PAIR_W
    return [_col_spec(w_in, layer, PAIR_W, base + seg * N_PAIRS + p) for seg in range(3) for p in range(N_PAIRS)]


def _proj_diff(h, w_in, layer, depth, gq, gk, cos, sin, prev, nb, t, tm, is_prompt):
    m, d = h.shape
    if is_prompt:
        nt, grid, nseq, rows = t // tm, (nb, t // tm), 1, tm
    else:
        nt, grid, nseq, rows = 1, (1, 1), nb, t
    row = lambda b, i: (b * nt + i, 0)
    att = pl.BlockSpec((tm, ATT_W), row)
    rope = pl.BlockSpec((tm, LANES), lambda b, i: (i, 0))
    sspec, sshape = _state_out((depth, nb, N_HEADS, t, HEAD_DIM), layer, nseq, rows)
    out_specs = [att, sspec, sspec]
    out_shape = [jax.ShapeDtypeStruct((m, ATT_W), BF16 if is_prompt else F32), sshape, sshape]
    if is_prompt:
        aspecs, ashapes = _attn_operand_out(nb, t, tm, HEAD_DIM)
        out_specs += aspecs
        out_shape += ashapes
    in_specs = ([pl.BlockSpec((tm, d), row)] + _qkv_specs(w_in, layer, DIFF_COL0)
                + [_full_spec((1, LANES)), _full_spec((1, LANES)), rope, rope])
    alias_in, alias_specs, aliases = _alias_args(prev, len(in_specs), 1)
    return pl.pallas_call(
        functools.partial(_proj_diff_kernel, nseq=nseq, rows=rows, n_alias=len(alias_in), attn_ops=is_prompt),
        name="proj_diff", grid=grid, in_specs=in_specs + alias_specs, out_specs=out_specs, out_shape=out_shape,
        input_output_aliases=aliases,
        compiler_params=_cparams(2, 44 << 20),
    )(h, *([w_in] * (3 * N_PAIRS)), gq, gk, cos, sin, *alias_in)


def _scan_lanes(x, seg):
    lane = lax.broadcasted_iota(jnp.int32, x.shape, 1)
    pos = lane & (seg - 1)
    s = 1
    while s < seg:
        x = x + jnp.where(pos >= s, pltpu.roll(x, s, 1), 0.0)
        s *= 2
    return x


def _proj_fox_kernel(*refs, nseq, rows, seg, n_alias, attn_ops):
    h_ref = refs[0]
    wq_refs, wk_refs, wv_refs = refs[1:1 + N_PAIRS], refs[1 + N_PAIRS:1 + 2 * N_PAIRS], refs[1 + 2 * N_PAIRS:N_W]
    wf_ref, fb_ref, gq_ref, gk_ref = refs[N_W:N_W + 4]
    outs = refs[N_W + 4 + n_alias:-1]
    carry_ref = refs[-1]
    q_ref, ks_ref, vs_ref, lf_ref, bias_ref = outs[:5]
    hb = h_ref[...]
    tm = hb.shape[0]
    z = _dot(hb, wf_ref[...]) + fb_ref[...]
    lf = jnp.minimum(z, 0.0) - jnp.log1p(jnp.exp(-jnp.abs(z)))
    if tm < LANES:
        lf = jnp.concatenate([lf, jnp.zeros((LANES - tm, LANES), F32)], axis=0)
    tb = lf.shape[0]
    lft = lf.T[:SUBLANES, :]
    lf_ref[...] = lft
    blocks = []
    if seg >= LANES:
        @pl.when(pl.program_id(1) == 0)
        def _():
            carry_ref[...] = jnp.zeros_like(carry_ref)
        c = carry_ref[...]
        for kb in range(tb // LANES):
            blk = _scan_lanes(lft[:, kb * LANES:(kb + 1) * LANES], LANES) + c
            blocks.append(-blk)
            c = jnp.broadcast_to(blk[:, LANES - 1:LANES], blk.shape)
        carry_ref[...] = c
    else:
        for kb in range(tb // LANES):
            blocks.append(-_scan_lanes(lft[:, kb * LANES:(kb + 1) * LANES], seg))
    bias = jnp.concatenate(blocks, axis=1) if len(blocks) > 1 else blocks[0]
    bias_ref[...] = bias
    if attn_ops:
        bias_t = jnp.concatenate([bias, jnp.zeros((LANES - SUBLANES, tb), F32)], axis=0).T
        lane = lax.broadcasted_iota(jnp.int32, (tm, LANES), 1)

    scale = HEAD_DIM ** -0.5 * (LOG2E if attn_ops else 1.0)
    for pair in range(N_PAIRS):
        zq = _dot(hb, wq_refs[pair][...])
        zk = _dot(hb, wk_refs[pair][...])
        zv = _dot(hb, wv_refs[pair][...])
        for sub in range(2):
            hh = 2 * pair + sub
            sl = slice(sub * HEAD_DIM, (sub + 1) * HEAD_DIM)
            x = zq[:, sl]
            q = x * lax.rsqrt(jnp.mean(x * x, axis=-1, keepdims=True) + EPS) * gq_ref[...] * scale
            q_ref[:, hh * HEAD_DIM:(hh + 1) * HEAD_DIM] = q.astype(q_ref.dtype)
            x = zk[:, sl]
            k = x * lax.rsqrt(jnp.mean(x * x, axis=-1, keepdims=True) + EPS) * gk_ref[...]
            v = zv[:, sl]
            _store_heads(ks_ref, hh, k, nseq, rows)
            _store_heads(vs_ref, hh, v, nseq, rows)
            if attn_ops:
                kb_ref, vt_ref = outs[5:7]
                hi, mid, lo = _split3(jnp.broadcast_to(bias_t[:, hh:hh + 1], (tm, LANES)) * LOG2E)
                aug = jnp.where(lane == 0, hi, jnp.where(lane == 1, mid, jnp.where(lane == 2, lo, 0.0)))
                kb_ref[0, hh, :, 0:HEAD_DIM] = k.astype(BF16)
                kb_ref[0, hh, :, HEAD_DIM:2 * HEAD_DIM] = aug.astype(BF16)
                vt_ref[0, hh] = v.T.astype(BF16)


def _proj_fox(h, w_in, wf, layer, depth, fb, gq, gk, prev, nb, t, tm, is_prompt):
    m, d = h.shape
    if is_prompt:
        nt, grid, nseq, rows, ng = t // tm, (nb, t // tm), 1, tm, nb
    else:
        nt, grid, nseq, rows, ng = 1, (1, 1), nb, t, 1
    tb = max(tm, LANES)
    row = lambda b, i: (b * nt + i, 0)
    att = pl.BlockSpec((tm, ATT_W), row)
    tspec = pl.BlockSpec((None, SUBLANES, tb), lambda b, i: (b, 0, i))
    tshape = jax.ShapeDtypeStruct((ng, SUBLANES, nt * tb), F32)
    sspec, sshape = _state_out((depth, nb, N_HEADS, t, HEAD_DIM), layer, nseq, rows)
    out_specs = [att, sspec, sspec, tspec, tspec]
    out_shape = [jax.ShapeDtypeStruct((m, ATT_W), BF16 if is_prompt else F32), sshape, sshape, tshape, tshape]
    if is_prompt:
        aspecs, ashapes = _attn_operand_out(nb, t, tm, 2 * HEAD_DIM)
        out_specs += aspecs
        out_shape += ashapes
    in_specs = ([pl.BlockSpec((tm, d), row)] + _qkv_specs(w_in, layer, FOX_COL0)
                + [_layer_spec(wf, layer), _full_spec((1, LANES)), _full_spec((1, LANES)), _full_spec((1, LANES))])
    alias_in, alias_specs, aliases = _alias_args(prev, len(in_specs), 1)
    return pl.pallas_call(
        functools.partial(_proj_fox_kernel, nseq=nseq, rows=rows, seg=t, n_alias=len(alias_in),
                          attn_ops=is_prompt),
        name="proj_fox", grid=grid, in_specs=in_specs + alias_specs, out_specs=out_specs, out_shape=out_shape,
        input_output_aliases=aliases,
        scratch_shapes=[pltpu.VMEM((SUBLANES, LANES), F32)],
        compiler_params=_cparams(2, 44 << 20),
    )(h, *([w_in] * (3 * N_PAIRS)), wf, fb, gq, gk, *alias_in)


def _lambda(lq1_ref, lk1_ref, lq2_ref, lk2_ref, lam_init):
    a = jnp.sum(lq1_ref[...] * lk1_ref[...], axis=-1, keepdims=True)
    b = jnp.sum(lq2_ref[...] * lk2_ref[...], axis=-1, keepdims=True)
    return jnp.exp(a) - jnp.exp(b) + lam_init


def _diff_combine(a1, l1, a2, l2, lam, g_sub, lam_init):
    y = a1 / l1 - lam * (a2 / l2)
    y = y * lax.rsqrt(jnp.mean(y * y, axis=-1, keepdims=True) + EPS) * g_sub
    return y * (1.0 - lam_init)


SOFTMAX_BANDS = 4


def _col_reduce(x, reduce_fn, combine_fn):
    band = x.shape[0] // SOFTMAX_BANDS
    parts = [reduce_fn(x[r * band:(r + 1) * band], axis=0, keepdims=True) for r in range(SOFTMAX_BANDS)]
    while len(parts) > 1:
        parts = [combine_fn(parts[2 * r], parts[2 * r + 1]) for r in range(len(parts) // 2)]
    return parts[0]


def _flash_t(i, tk, k_ref, vt_ref, qt, q_of_col, m_ref, acc_ref, sa_ref, sb_ref):
    m_ref[...] = jnp.full(m_ref.shape, NEG_INF, F32)
    acc_ref[...] = jnp.zeros(acc_ref.shape, F32)
    ones = jnp.ones((BF16_ROWS, tk), BF16)

    def scores(j, s_ref):
        start = pl.multiple_of(j * tk, tk)
        s_ref[...] = _dot(k_ref[pl.ds(start, tk), :], qt)

    def soft(j, s_ref, masked):
        start = pl.multiple_of(j * tk, tk)
        s = s_ref[...]
        if masked:
            key = lax.broadcasted_iota(jnp.int32, s.shape, 0)
            s = jnp.where(key <= q_of_col, s, NEG_INF)
        m_old = m_ref[...]
        m_new = jnp.maximum(m_old, _col_reduce(s, jnp.max, jnp.maximum))
        alpha = jnp.exp2(m_old - m_new)
        p = jnp.exp2(s - m_new)
        vt = jnp.concatenate([vt_ref[:, pl.ds(start, tk)], ones], axis=0)
        acc_ref[...] = alpha * acc_ref[...] + _dot(vt, p.astype(BF16))
        m_ref[...] = m_new

    scores(0, sa_ref)

    def body(jj, c):
        j = 2 * jj
        scores(j + 1, sb_ref)
        soft(j, sa_ref, False)
        scores(j + 2, sa_ref)
        soft(j + 1, sb_ref, False)
        return c

    lax.fori_loop(0, lax.shift_right_logical(i, 1), body, 0)
    odd = (i & 1) == 1

    @pl.when(odd)
    def _():
        scores(i, sb_ref)
        soft(i - 1, sa_ref, False)
        soft(i, sb_ref, True)

    @pl.when(jnp.logical_not(odd))
    def _():
        soft(i, sa_ref, True)


def _normalized(acc_ref):
    return acc_ref[0:HEAD_DIM, :] * (1.0 / acc_ref[HEAD_DIM:HEAD_DIM + 1, :])


def _diff_attn_kernel(q_ref, k_ref, vt_ref, lq1_ref, lk1_ref, lq2_ref, lk2_ref, gs_ref, o_ref,
                      m_ref, acc_ref, sa_ref, sb_ref, *, tq, lam_init):
    qt = q_ref[...].astype(F32).T
    sub = lax.broadcasted_iota(jnp.int32, qt.shape, 0)
    qt2 = jnp.concatenate([jnp.where(sub < DK_DIFF, qt, 0.0), jnp.where(sub < DK_DIFF, 0.0, qt)],
                          axis=1).astype(BF16)
    col = lax.broadcasted_iota(jnp.int32, (1, 2 * tq), 1) & (tq - 1)
    _flash_t(pl.program_id(2), tq, k_ref, vt_ref, qt2, col, m_ref, acc_ref, sa_ref, sb_ref)
    lam = _lambda(lq1_ref, lk1_ref, lq2_ref, lk2_ref, lam_init)
    o = _normalized(acc_ref)
    y = (o[:, :tq] - lam * o[:, tq:]).T
    y = y * lax.rsqrt(jnp.mean(y * y, axis=-1, keepdims=True) + EPS) * gs_ref[...]
    o_ref[...] = (y * (1.0 - lam_init)).astype(o_ref.dtype)


def _fox_attn_kernel(q_ref, k_ref, vt_ref, o_ref, m_ref, acc_ref, sa_ref, sb_ref, *, tq):
    qt = q_ref[...].astype(F32).T
    sub = lax.broadcasted_iota(jnp.int32, qt.shape, 0)
    ones = jnp.where(sub < BIAS_PARTS, 1.0, 0.0)
    qa = jnp.concatenate([qt, ones], axis=0).astype(BF16)
    col = lax.broadcasted_iota(jnp.int32, (1, tq), 1)
    _flash_t(pl.program_id(2), tq, k_ref, vt_ref, qa, col, m_ref, acc_ref, sa_ref, sb_ref)
    o_ref[...] = _normalized(acc_ref).T.astype(o_ref.dtype)


def _attn_call(kern, name, q, k, vt, extra, extra_specs, nb, t, tq, ncols):
    assert tq & (tq - 1) == 0 and tq % (SOFTMAX_BANDS * SUBLANES) == 0
    nt = t // tq
    kd = k.shape[-1]
    qspec = pl.BlockSpec((tq, HEAD_DIM), lambda b, h, i: (b * nt + i, h))
    score = pltpu.VMEM((tq, ncols), F32)
    return pl.pallas_call(
        kern, name=name,
        grid=(nb, N_HEADS, nt),
        in_specs=[qspec,
                  pl.BlockSpec((None, None, t, kd), lambda b, h, i: (b, h, 0, 0)),
                  pl.BlockSpec((None, None, HEAD_DIM, t), lambda b, h, i: (b, h, 0, 0))] + extra_specs,
        out_specs=qspec,
        out_shape=jax.ShapeDtypeStruct(q.shape, BF16),
        scratch_shapes=[pltpu.VMEM((1, ncols), F32), pltpu.VMEM((HEAD_DIM + BF16_ROWS, ncols), F32),
                        score, score],
        compiler_params=_cparams(3, 40 << 20),
    )(q, k, vt, *extra)


def _diff_attn(q, k, vt, lq1, lk1, lq2, lk2, g_sub, nb, t, tq, lam_init):
    vec = pl.BlockSpec((1, DK_DIFF), lambda b, h, i: (0, 0))
    return _attn_call(functools.partial(_diff_attn_kernel, tq=tq, lam_init=lam_init), "diff_attn", q, k, vt,
                      [lq1, lk1, lq2, lk2, g_sub],
                      [vec, vec, vec, vec, pl.BlockSpec((1, HEAD_DIM), lambda b, h, i: (0, 0))],
                      nb, t, tq, 2 * tq)


def _fox_attn(q, k, vt, nb, t, tq):
    return _attn_call(functools.partial(_fox_attn_kernel, tq=tq), "fox_attn", q, k, vt, [], [], nb, t, tq, tq)


def _head_rows(q, width, n_groups):
    qt = jnp.concatenate([q] * n_groups, axis=0)
    lane = lax.broadcasted_iota(jnp.int32, qt.shape, 1)
    grp = jnp.right_shift(lax.broadcasted_iota(jnp.int32, qt.shape, 0), 3)
    lo = grp * width
    return jnp.where((lane >= lo) & (lane < lo + width), qt, 0.0).astype(BF16)


def _expand_heads(x, n):
    return jnp.concatenate([jnp.broadcast_to(x[h:h + 1, :], (SUBLANES, n)) for h in range(N_HEADS)], axis=0)


def _flat_heads(ref):
    return jnp.concatenate([ref[h].astype(BF16) for h in range(N_HEADS)], axis=1)


def _dec_update(s, v_tiles, m_ref, l_ref, acc_ref):
    m_old = m_ref[...]
    m_new = jnp.maximum(m_old, jnp.max(s, axis=-1, keepdims=True))
    alpha = jnp.exp(m_old - m_new)
    p = jnp.exp(s - m_new)
    l_ref[...] = alpha * l_ref[...] + jnp.sum(p, axis=-1, keepdims=True)
    pb = p.astype(BF16)
    pv = None
    for r, v in enumerate(v_tiles):
        d = _dot(pb[:, r * LANES:(r + 1) * LANES], v)
        pv = d if pv is None else pv + d
    acc_ref[...] = alpha * acc_ref[...] + pv
    m_ref[...] = m_new


def _new_tokens(qr, kn_ref, vn_ref, bias_rows):
    def padded(ref):
        x = jnp.concatenate([ref[h] for h in range(N_HEADS)], axis=1)
        return jnp.concatenate([x, jnp.zeros((LANES - SUBLANES, ATT_W), F32)], axis=0).astype(BF16)

    s = _dot_nt(qr, padded(kn_ref))
    if bias_rows is not None:
        s = s + bias_rows
    keep = (lax.broadcasted_iota(jnp.int32, s.shape, 1)
            <= (lax.broadcasted_iota(jnp.int32, s.shape, 0) & (SUBLANES - 1)))
    return jnp.where(keep, s, NEG_INF), padded(vn_ref)


def _dec_init(q_ref, qr_ref, m_ref, l_ref, acc_ref, width, n_groups):
    qr_ref[...] = _head_rows(q_ref[...], width, n_groups)
    m_ref[...] = jnp.full_like(m_ref, NEG_INF)
    l_ref[...] = jnp.zeros_like(l_ref)
    acc_ref[...] = jnp.zeros_like(acc_ref)


def _diff_dec_kernel(pt_ref, q_ref, kn_ref, vn_ref, lq1_ref, lk1_ref, lq2_ref, lk2_ref, gs_ref, *rest,
                     npg, lam_init):
    k_refs = rest[:npg]
    v_refs = rest[npg:2 * npg]
    o_ref, qr_ref, m_ref, l_ref, acc_ref = rest[2 * npg:]
    j = pl.program_id(1)

    @pl.when(j == 0)
    def _():
        _dec_init(q_ref, qr_ref, m_ref, l_ref, acc_ref, DK_DIFF, 2 * N_HEADS)

    qr = qr_ref[...]
    s = jnp.concatenate([_dot_nt(qr, _flat_heads(k)) for k in k_refs], axis=1)
    _dec_update(s, [_flat_heads(v) for v in v_refs], m_ref, l_ref, acc_ref)

    @pl.when(j == pl.num_programs(1) - 1)
    def _():
        sn, vn = _new_tokens(qr, kn_ref, vn_ref, None)
        _dec_update(sn, [vn], m_ref, l_ref, acc_ref)
        lam = _lambda(lq1_ref, lk1_ref, lq2_ref, lk2_ref, lam_init)
        acc = acc_ref[...]
        l = l_ref[...]
        for h in range(N_HEADS):
            r1 = slice(2 * h * SUBLANES, (2 * h + 1) * SUBLANES)
            r2 = slice((2 * h + 1) * SUBLANES, (2 * h + 2) * SUBLANES)
            cl = slice(h * HEAD_DIM, (h + 1) * HEAD_DIM)
            o_ref[:, cl] = _diff_combine(acc[r1, cl], l[r1], acc[r2, cl], l[r2], lam, gs_ref[...], lam_init)


def _fox_dec_kernel(pt_ref, q_ref, kn_ref, vn_ref, bn_ref, tri_ref, *rest, npg):
    k_refs = rest[:npg]
    v_refs = rest[npg:2 * npg]
    lf_refs = rest[2 * npg:3 * npg]
    o_ref, qr_ref, m_ref, l_ref, acc_ref, carry_ref = rest[3 * npg:]
    j = pl.program_id(1)

    @pl.when(j == 0)
    def _():
        _dec_init(q_ref, qr_ref, m_ref, l_ref, acc_ref, HEAD_DIM, N_HEADS)
        carry_ref[...] = jnp.zeros_like(carry_ref)

    terms = []
    for lf in lf_refs:
        terms += list(_split3(lf[...]))
    y = _dot(jnp.concatenate(terms, axis=0).astype(BF16), tri_ref[...])
    c = carry_ref[...]
    biases = []
    rows = BIAS_PARTS * SUBLANES
    for r in range(npg):
        yr = y[r * rows:(r + 1) * rows]
        blk = (yr[0:SUBLANES] + yr[SUBLANES:2 * SUBLANES]) + yr[2 * SUBLANES:3 * SUBLANES] + c
        biases.append(-blk)
        c = jnp.broadcast_to(blk[:, LANES - 1:LANES], blk.shape)
    carry_ref[...] = c

    qr = qr_ref[...]
    s = jnp.concatenate([_dot_nt(qr, _flat_heads(k)) for k in k_refs], axis=1)
    s = s + _expand_heads(jnp.concatenate(biases, axis=1), npg * LANES)
    _dec_update(s, [_flat_heads(v) for v in v_refs], m_ref, l_ref, acc_ref)

    @pl.when(j == pl.num_programs(1) - 1)
    def _():
        sn, vn = _new_tokens(qr, kn_ref, vn_ref, _expand_heads(bn_ref[...] - c, LANES))
        _dec_update(sn, [vn], m_ref, l_ref, acc_ref)
        acc = acc_ref[...]
        l = l_ref[...]
        for h in range(N_HEADS):
            rs = slice(h * SUBLANES, (h + 1) * SUBLANES)
            cl = slice(h * HEAD_DIM, (h + 1) * HEAD_DIM)
            o_ref[:, cl] = acc[rs, cl] / l[rs]


def _page_specs(layer, n_pages, npg, block):
    nz = (0,) * (len(block) - 2)

    def spec(r):
        return pl.BlockSpec(block, lambda b, j, pt, r=r: (layer, pt[b * n_pages + j * npg + r]) + nz)
    return [spec(r) for r in range(npg)]


def _dec_common(layer, tnew, rows):
    tok = pl.BlockSpec((tnew, ATT_W), lambda b, j, pt: (b, 0))
    new_kv = pl.BlockSpec((None, None, N_HEADS, tnew, HEAD_DIM), lambda b, j, pt: (layer, b, 0, 0, 0))
    scratch = [pltpu.VMEM((rows, ATT_W), BF16), pltpu.VMEM((rows, 1), F32), pltpu.VMEM((rows, 1), F32),
               pltpu.VMEM((rows, ATT_W), F32)]
    return tok, new_kv, scratch


def _diff_decode(pt, q, kn, vn, lq1, lk1, lq2, lk2, g_sub, cache_k, cache_v, layer, nb, n_pages, lam_init):
    tnew = q.shape[0] // nb
    assert tnew == SUBLANES
    page = cache_k.shape[3]
    npg = _tile(n_pages, DEC_PAGES_PER_STEP)
    tok, new_kv, scratch = _dec_common(layer, tnew, 2 * N_HEADS * SUBLANES)
    vec = pl.BlockSpec((1, DK_DIFF), lambda b, j, pt: (0, 0))
    pages = _page_specs(layer, n_pages, npg, (None, None, N_HEADS, page, HEAD_DIM))
    return pl.pallas_call(
        functools.partial(_diff_dec_kernel, npg=npg, lam_init=lam_init), name="diff_decode",
        grid_spec=pltpu.PrefetchScalarGridSpec(
            num_scalar_prefetch=1, grid=(nb, n_pages // npg),
            in_specs=[tok, new_kv, new_kv, vec, vec, vec, vec,
                      pl.BlockSpec((1, HEAD_DIM), lambda b, j, pt: (0, 0))] + pages + pages,
            out_specs=tok, scratch_shapes=scratch),
        out_shape=jax.ShapeDtypeStruct(q.shape, F32),
        compiler_params=_cparams(2, 4 * npg * math.prod(cache_k.shape[2:]) * 4 + (20 << 20)),
    )(pt, q, kn, vn, lq1, lk1, lq2, lk2, g_sub, *([cache_k] * npg), *([cache_v] * npg))


def _fox_decode(pt, q, kn, vn, bias_new, cache_k, cache_v, cache_lf, layer, nb, n_pages):
    tnew = q.shape[0] // nb
    assert tnew == SUBLANES
    page = cache_k.shape[3]
    assert page == LANES
    npg = _tile(n_pages, DEC_PAGES_PER_STEP)
    tok, new_kv, scratch = _dec_common(layer, tnew, N_HEADS * SUBLANES)
    pages = _page_specs(layer, n_pages, npg, (None, None, N_HEADS, page, HEAD_DIM))
    lf_pages = _page_specs(layer, n_pages, npg, (None, None, SUBLANES, page))
    idx = jnp.arange(page, dtype=jnp.int32)
    tri = (idx[:, None] <= idx[None, :]).astype(BF16)
    return pl.pallas_call(
        functools.partial(_fox_dec_kernel, npg=npg), name="fox_decode",
        grid_spec=pltpu.PrefetchScalarGridSpec(
            num_scalar_prefetch=1, grid=(nb, n_pages // npg),
            in_specs=[tok, new_kv, new_kv,
                      pl.BlockSpec((None, SUBLANES, LANES), lambda b, j, pt: (b, 0, 0)),
                      pl.BlockSpec((page, page), lambda b, j, pt: (0, 0))]
            + pages + pages + lf_pages,
            out_specs=tok, scratch_shapes=scratch + [pltpu.VMEM((SUBLANES, LANES), F32)]),
        out_shape=jax.ShapeDtypeStruct(q.shape, F32),
        compiler_params=_cparams(2, 4 * npg * math.prod(cache_k.shape[2:]) * 4 + (20 << 20)),
    )(pt, q, kn, vn, bias_new, tri, *([cache_k] * npg), *([cache_v] * npg), *([cache_lf] * npg))


def _out_proj_kernel(yc_ref, yd_ref, yf_ref, w_ref, x_ref, g1_ref, gn_ref, sc_ref, sh_ref, x1_ref, h2_ref):
    mix = jnp.concatenate([yc_ref[...].astype(BF16), yd_ref[...].astype(BF16), yf_ref[...].astype(BF16)], axis=1)
    x1 = x_ref[...] + g1_ref[...] * _dot(mix, w_ref[...])
    x1_ref[...] = x1
    h2_ref[...] = _norm_mod_value(x1, gn_ref[...], sc_ref[...], sh_ref[...]).astype(h2_ref.dtype)


def _out_proj(yc, yd, yf, w_out, layer, x, g1, gn, sc2, sh2, tm, tiles_per_group):
    m, d = x.shape
    row = lambda i: (i, 0)
    return pl.pallas_call(
        _out_proj_kernel, name="out_proj",
        grid=(m // tm,),
        in_specs=[pl.BlockSpec((tm, C_CONV), row), pl.BlockSpec((tm, ATT_W), row), pl.BlockSpec((tm, ATT_W), row),
                  _layer_spec(w_out, layer), pl.BlockSpec((tm, d), row), _mod_spec(g1, tiles_per_group),
                  _full_spec((1, d)), _mod_spec(sc2, tiles_per_group), _mod_spec(sh2, tiles_per_group)],
        out_specs=[pl.BlockSpec((tm, d), row), pl.BlockSpec((tm, d), row)],
        out_shape=[jax.ShapeDtypeStruct((m, d), F32), jax.ShapeDtypeStruct((m, d), BF16)],
        compiler_params=_cparams(1, 40 << 20),
    )(yc, yd, yf, w_out, x, g1, gn, sc2, sh2)


def _ffn_up_kernel(h_ref, wa_ref, wb_ref, g_ref):
    hb = h_ref[...]
    a = _dot(hb, wa_ref[...].astype(BF16))
    b = _dot(hb, wb_ref[...].astype(BF16))
    g_ref[...] = (a * jax.nn.sigmoid(a) * b).astype(g_ref.dtype)


def _ffn_up(h2, w_in, layer, tm, tf):
    m, d = h2.shape
    f = w_in.shape[2] // 2
    nf = f // tf
    return pl.pallas_call(
        _ffn_up_kernel, name="ffn_up",
        grid=(m // tm, nf),
        in_specs=[pl.BlockSpec((tm, d), lambda i, j: (i, 0)),
                  pl.BlockSpec((None, d, tf), lambda i, j: (layer, 0, j)),
                  pl.BlockSpec((None, d, tf), lambda i, j: (layer, 0, j + nf))],
        out_specs=pl.BlockSpec((tm, tf), lambda i, j: (i, j)),
        out_shape=jax.ShapeDtypeStruct((m, f), BF16),
        compiler_params=_cparams(2, 40 << 20),
    )(h2, w_in, w_in)


def _ffn_down_kernel(g_ref, w_ref, x_ref, g2_ref, *rest, next_norm):
    x2 = x_ref[...] + g2_ref[...] * _dot(g_ref[...], w_ref[...])
    if next_norm:
        gn_ref, sc_ref, sh_ref, x2_ref, hn_ref = rest
        hn_ref[...] = _norm_mod_value(x2, gn_ref[...], sc_ref[...], sh_ref[...]).astype(hn_ref.dtype)
    else:
        x2_ref, = rest
    x2_ref[...] = x2


def _ffn_down(g, w_out, layer, x1, g2, nxt, tm, tiles_per_group):
    m, f = g.shape
    d = w_out.shape[2]
    xspec = pl.BlockSpec((tm, d), lambda i: (i, 0))
    in_specs = [pl.BlockSpec((tm, f), lambda i: (i, 0)),
                pl.BlockSpec((None, f, d), lambda i: (layer, 0, 0), pipeline_mode=pl.Buffered(1)),
                xspec, _mod_spec(g2, tiles_per_group)]
    args = [g, w_out, x1, g2]
    out_specs, out_shape = [xspec], [jax.ShapeDtypeStruct((m, d), F32)]
    if nxt is not None:
        in_specs += [_full_spec((1, d)), _mod_spec(nxt[1], tiles_per_group), _mod_spec(nxt[2], tiles_per_group)]
        args += list(nxt)
        out_specs.append(xspec)
        out_shape.append(jax.ShapeDtypeStruct((m, d), BF16))
    out = pl.pallas_call(
        functools.partial(_ffn_down_kernel, next_norm=nxt is not None), name="ffn_down",
        grid=(m // tm,),
        in_specs=in_specs, out_specs=out_specs, out_shape=out_shape,
        compiler_params=_cparams(1, f * d * 2 + 6 * tm * (f + 4 * d) + (8 << 20)),
    )(*args)
    return (out[0], out[1]) if nxt is not None else (out[0], None)


def _rope_tables(pos):
    half = DK_DIFF // 2
    inv = ROPE_THETA ** (-jnp.arange(half, dtype=F32) / half)
    ang = pos.astype(F32)[:, None] * inv[None, :]
    cos = jnp.tile(jnp.cos(ang), (1, LANES // half))
    sin = jnp.sin(ang)
    sin = jnp.tile(jnp.concatenate([-sin, sin], axis=1), (1, LANES // DK_DIFF))
    return cos, sin


class _Group:
    def __init__(self, nb, t, is_prompt):
        self.nb, self.t, self.is_prompt = nb, t, is_prompt
        self.m = nb * t
        if is_prompt:
            self.tm = _tile(t, 256)
            self.tm_out = _tile(t, 512)
            self.tm_conv = _tile(t, 512)
            self.tm_ffn = _tile(t, 1024)
            self.tm_down = _tile(t, 256)
            self.tq = _tile(t, 512)
            self.down_tiles = t // self.tm_down
            self.out_tiles = t // self.tm_out
        else:
            self.tm = self.tm_out = self.tm_conv = self.tm_ffn = self.tm_down = self.m
            self.down_tiles = self.out_tiles = 1


def _mods(ada, grp):
    d = ada.shape[1] // 6
    parts = jnp.split(ada, 6, axis=-1)
    if grp.is_prompt:
        return [p.reshape(grp.nb, 1, d) for p in parts]
    return [jnp.repeat(p, grp.t, axis=0).reshape(1, grp.m, d) for p in parts]


def _layer(x, h, grp, wts, lw, layer, depth, mods, nxt, rope, lam_init, states, ctx):
    sh1, sc1, g1, sh2, sc2, g2 = mods
    cos, sin = rope
    nb, t = grp.nb, grp.t
    prompt = grp.is_prompt
    prev_d, prev_f = states[0:2], states[2:4]

    yc, conv_new = _proj_conv(h, wts["w_in"], layer, lw["conv_w"], lw["conv_b"], lw["conv_ln_g"],
                              lw["conv_ln_b"], None if prompt else ctx["state_pad"],
                              nb, t, grp.tm_conv, BF16 if prompt else F32)
    dres = _proj_diff(h, wts["w_in"], layer, depth, lw["diff_qn"], lw["diff_kn"],
                      cos, sin, prev_d, nb, t, grp.tm, prompt)
    fres = _proj_fox(h, wts["w_in"], wts["w_ff"], layer, depth, lw["fox_fb"],
                     lw["fox_qn"], lw["fox_kn"], prev_f, nb, t, grp.tm, prompt)
    qd, kd, vd = dres[:3]
    qf, kf, vf, lft, fbias = fres[:5]
    if prompt:
        yd = _diff_attn(qd, dres[3], dres[4], lw["lam_q1"], lw["lam_k1"], lw["lam_q2"], lw["lam_k2"],
                        lw["diff_subln"], nb, t, grp.tq, lam_init)
        yf = _fox_attn(qf, fres[5], fres[6], nb, t, grp.tq)
    else:
        yd = _diff_decode(ctx["pt"], qd, kd, vd, lw["lam_q1"], lw["lam_k1"], lw["lam_q2"], lw["lam_k2"],
                          lw["diff_subln"], ctx["cache_dk"], ctx["cache_dv"], layer, nb, ctx["n_pages"], lam_init)
        bn = jnp.transpose(fbias[0, :, :nb * t].reshape(SUBLANES, nb, t), (1, 0, 2))
        bn = jnp.pad(bn, ((0, 0), (0, 0), (0, LANES - t)))
        yf = _fox_decode(ctx["pt"], qf, kf, vf, bn, ctx["cache_fk"], ctx["cache_fv"], ctx["cache_lf"],
                         layer, nb, ctx["n_pages"])
    x1, h2 = _out_proj(yc, yd, yf, wts["w_out"], layer, x, g1, lw["norm_ffn"], sc2, sh2, grp.tm_out,
                       grp.out_tiles)
    g = _ffn_up(h2, wts["w_ffn_in"], layer, grp.tm_ffn, wts["tf"])
    x2, h_next = _ffn_down(g, wts["w_ffn_out"], layer, x1, g2, nxt, grp.tm_down, grp.down_tiles)
    return x2, h_next, (kd, vd, kf, vf), lft, conv_new


def kernel(x_prompt, x_sample, cache_diff_k, cache_diff_v, cache_fox_k, cache_fox_v, cache_fox_lf, state_conv,
           page_table, c_prompt, c_sample, norm_mix, norm_ffn, w_ada, b_ada, w_in, conv_w, conv_b, conv_ln_g,
           conv_ln_b, diff_qn, diff_kn, lam_q1, lam_k1, lam_q2, lam_k2, diff_subln, fox_qn, fox_kn, fox_fb,
           w_out, w_ffn_in, w_ffn_out):
    depth = w_in.shape[0]
    bp, tp, d = x_prompt.shape
    bs, ts, _ = x_sample.shape
    page = cache_diff_k.shape[2]
    n_pages = page_table.shape[1]
    d_ff = w_ffn_out.shape[1]
    assert d_ff % LANES == 0
    gp = _Group(bp, tp, True)
    gs = _Group(bs, ts, False)

    c_all = jnp.concatenate([c_prompt, c_sample], axis=0)
    n_c = c_all.shape[0]
    c_all = jnp.pad(c_all, ((0, (-n_c) % (2 * SUBLANES)), (0, 0)))
    ada = _ada(c_all, w_ada, b_ada)
    mods_p = [_mods(ada[l, :bp], gp) for l in range(depth)]
    mods_s = [_mods(ada[l, bp:bp + bs], gs) for l in range(depth)]

    n_main = FOX_COL0 + 3 * ATT_W
    wts = {"w_in": w_in[:, :, :n_main].astype(BF16)}
    wts["w_ff"] = jnp.pad(w_in[:, :, n_main:], ((0, 0), (0, 0), (0, LANES - N_HEADS))).astype(BF16)
    wts["w_out"] = w_out.astype(BF16)
    wts["w_ffn_in"] = w_ffn_in
    wts["w_ffn_out"] = w_ffn_out.astype(BF16)
    wts["tf"] = _tile(d_ff, 512)

    rope_p = _rope_tables(jnp.arange(tp, dtype=jnp.int32))
    cos_s, sin_s = _rope_tables(n_pages * page + jnp.arange(ts, dtype=jnp.int32))
    rope_s = (jnp.tile(cos_s, (bs, 1)), jnp.tile(sin_s, (bs, 1)))

    head_major = lambda c: jnp.transpose(c, (0, 1, 3, 2, 4))
    ctx = dict(pt=page_table.reshape(-1).astype(jnp.int32), n_pages=n_pages,
               cache_dk=head_major(cache_diff_k), cache_dv=head_major(cache_diff_v),
               cache_fk=head_major(cache_fox_k), cache_fv=head_major(cache_fox_v),
               cache_lf=jnp.pad(jnp.transpose(cache_fox_lf, (0, 1, 3, 2)),
                                ((0, 0), (0, 0), (0, SUBLANES - N_HEADS), (0, 0))))
    state_pad = jnp.pad(state_conv, ((0, 0), (0, 0), (HIST_PAD - CONV_HIST, 0), (0, 0)))

    xp = x_prompt.reshape(bp * tp, d)
    xs = x_sample.reshape(bs * ts, d)
    g0 = norm_mix[0][None, :]
    hp = _norm_mod(xp, g0, mods_p[0][1], mods_p[0][0], gp.tm_down, gp.down_tiles)
    hs = _norm_mod(xs, g0, mods_s[0][1], mods_s[0][0], gs.tm, 1)
    kv_p = tuple(jnp.zeros((depth, bp, N_HEADS, tp, HEAD_DIM), F32) for _ in range(4))
    kv_s = tuple(jnp.zeros((depth, bs, N_HEADS, ts, HEAD_DIM), F32) for _ in range(4))
    lf_p, lf_s, conv_p, conv_s = [], [], [], []
    for l in range(depth):
        lam_init = 0.8 - 0.6 * math.exp(-0.3 * l)
        lw = dict(
            conv_w=conv_w[l], conv_b=conv_b[l][None, :], conv_ln_g=conv_ln_g[l][None, :],
            conv_ln_b=conv_ln_b[l][None, :],
            diff_qn=jnp.tile(diff_qn[l], LANES // DK_DIFF)[None, :],
            diff_kn=jnp.tile(diff_kn[l], LANES // DK_DIFF)[None, :],
            lam_q1=lam_q1[l][None, :], lam_k1=lam_k1[l][None, :], lam_q2=lam_q2[l][None, :],
            lam_k2=lam_k2[l][None, :], diff_subln=diff_subln[l][None, :],
            fox_qn=fox_qn[l][None, :], fox_kn=fox_kn[l][None, :],
            fox_fb=jnp.pad(fox_fb[l], (0, LANES - N_HEADS))[None, :],
            norm_ffn=norm_ffn[l][None, :])
        last = l == depth - 1
        nxt_p = None if last else (norm_mix[l + 1][None, :], mods_p[l + 1][1], mods_p[l + 1][0])
        nxt_s = None if last else (norm_mix[l + 1][None, :], mods_s[l + 1][1], mods_s[l + 1][0])
        xp, hp, kv_p, lft, cn = _layer(xp, hp, gp, wts, lw, l, depth, mods_p[l], nxt_p, rope_p, lam_init, kv_p,
                                       None)
        lf_p.append(lft)
        conv_p.append(cn)
        xs, hs, kv_s, lft, cn = _layer(xs, hs, gs, wts, lw, l, depth, mods_s[l], nxt_s, rope_s, lam_init, kv_s,
                                       dict(ctx, state_pad=state_pad[l]))
        lf_s.append(lft)
        conv_s.append(cn)

    kv = lambda a: jnp.transpose(a, (0, 1, 3, 2, 4))
    lf_p = jnp.transpose(jnp.stack(lf_p, axis=0)[:, :, :N_HEADS, :], (0, 1, 3, 2))
    lf_s = jnp.stack(lf_s, axis=0)[:, 0, :N_HEADS, :bs * ts]
    lf_s = jnp.transpose(lf_s.reshape(depth, N_HEADS, bs, ts), (0, 2, 3, 1))
    return (xp.reshape(bp, tp, d), xs.reshape(bs, ts, d),
            kv(kv_p[0]), kv(kv_p[1]), kv(kv_p[2]), kv(kv_p[3]), lf_p, jnp.stack(conv_p, axis=0),
            kv(kv_s[0]), kv(kv_s[1]), kv(kv_s[2]), kv(kv_s[3]), lf_s, jnp.stack(conv_s, axis=0))
```

```python
import functools
import math

import jax
import jax.numpy as jnp
from jax import lax
from jax.experimental import pallas as pl
from jax.experimental.pallas import tpu as pltpu

F32 = jnp.float32
BF16 = jnp.bfloat16

EPS = 1e-6
NEG_INF = -1e30
ROPE_THETA = 10000.0
C_CONV = 512
CONV_WIDTH = 31
CONV_HIST = CONV_WIDTH - 1
N_HEADS = 6
HEAD_DIM = 128
DK_DIFF = 64
ATT_W = N_HEADS * HEAD_DIM
N_PAIRS = N_HEADS // 2
PAIR_W = 2 * HEAD_DIM
N_W = 1 + 3 * N_PAIRS
DIFF_COL0 = 2 * C_CONV
FOX_COL0 = DIFF_COL0 + 3 * ATT_W

LANES = 128
SUBLANES = 8
V7X_VMEM_BYTES = 64 * 1024 * 1024
VMEM_CAP = V7X_VMEM_BYTES - 8 * 1024 * 1024
HIST_PAD = 32
CONV_CHUNK = 64
BIAS_PARTS = 3
LOG2E = 1.4426950408889634
BF16_ROWS = 16

DEC_PAGES_PER_STEP = 16


def _dot(a, b):
    return jnp.dot(a, b, preferred_element_type=F32)


def _dot_nt(a, b):
    return lax.dot_general(a, b, (((1,), (1,)), ((), ())), preferred_element_type=F32)


def _cparams(n_grid, vmem_bytes):
    return pltpu.CompilerParams(dimension_semantics=("arbitrary",) * n_grid,
                                vmem_limit_bytes=int(min(max(vmem_bytes, 16 << 20), VMEM_CAP)))


def _tile(n, pref):
    t = min(n, pref)
    assert n % t == 0, (n, pref)
    return t


def _full_spec(shape):
    nd = len(shape)
    return pl.BlockSpec(shape, lambda *_: (0,) * nd)


def _layer_spec(w, layer):
    return pl.BlockSpec((None,) + tuple(w.shape[1:]), lambda *_: (layer, 0, 0), pipeline_mode=pl.Buffered(1))


def _col_spec(w, layer, width, block):
    return pl.BlockSpec((None, w.shape[1], width), lambda *_: (layer, 0, block), pipeline_mode=pl.Buffered(1))


def _mod_spec(mod, tiles_per_group):
    _, r, d = mod.shape
    return pl.BlockSpec((None, r, d), lambda i, *_: (i // tiles_per_group, 0, 0))


def _split3(x):
    hi = x.astype(BF16).astype(F32)
    r1 = x - hi
    mid = r1.astype(BF16).astype(F32)
    lo = (r1 - mid).astype(BF16).astype(F32)
    return hi, mid, lo


def _ada_kernel(c_ref, w_ref, b_ref, o_ref):
    c = c_ref[...]
    a = c * jax.nn.sigmoid(c)
    a_hi = a.astype(BF16)
    a_lo = (a - a_hi.astype(F32)).astype(BF16)
    w = w_ref[...].astype(BF16)
    o_ref[...] = _dot(a_hi, w) + _dot(a_lo, w) + b_ref[...]


def _ada(c_all, w_ada, b_ada):
    depth, d, n = w_ada.shape
    rows = c_all.shape[0]
    tn = _tile(n, 512)
    return pl.pallas_call(
        _ada_kernel, name="ada",
        grid=(depth, n // tn),
        in_specs=[pl.BlockSpec((rows, d), lambda l, j: (0, 0)),
                  pl.BlockSpec((None, d, tn), lambda l, j: (l, 0, j)),
                  pl.BlockSpec((None, 1, tn), lambda l, j: (l, 0, j))],
        out_specs=pl.BlockSpec((None, rows, tn), lambda l, j: (l, 0, j)),
        out_shape=jax.ShapeDtypeStruct((depth, rows, n), F32),
        compiler_params=_cparams(2, 6 * d * tn * 4),
    )(c_all, w_ada, b_ada.reshape(depth, 1, n))


def _norm_mod_value(x, g, sc, sh):
    y = x * lax.rsqrt(jnp.mean(x * x, axis=-1, keepdims=True) + EPS) * g
    return y * (1.0 + sc) + sh


def _norm_mod_kernel(x_ref, g_ref, sc_ref, sh_ref, h_ref):
    h_ref[...] = _norm_mod_value(x_ref[...], g_ref[...], sc_ref[...], sh_ref[...]).astype(h_ref.dtype)


def _norm_mod(x, g, sc, sh, tm, tiles_per_group):
    m, d = x.shape
    return pl.pallas_call(
        _norm_mod_kernel, name="norm_mod",
        grid=(m // tm,),
        in_specs=[pl.BlockSpec((tm, d), lambda i: (i, 0)), _full_spec((1, d)),
                  _mod_spec(sc, tiles_per_group), _mod_spec(sh, tiles_per_group)],
        out_specs=pl.BlockSpec((tm, d), lambda i: (i, 0)),
        out_shape=jax.ShapeDtypeStruct((m, d), BF16),
        compiler_params=_cparams(1, 8 * tm * d * 4),
    )(x, g, sc, sh)


def _conv_rows(ext_ref, z_ref, cw_ref, cb_ref, lg_ref, lb_ref, r0, n):
    acc = jnp.broadcast_to(cb_ref[...], (n, C_CONV))
    off = HIST_PAD - CONV_HIST
    for r in range(SUBLANES):
        span = n if r == 0 else n + SUBLANES
        z = None
        for a in range((CONV_WIDTH + off) // SUBLANES + 1):
            j = SUBLANES * a + r - off
            if 0 <= j < CONV_WIDTH:
                term = cw_ref[j:j + 1, :] * ext_ref[r0 + SUBLANES * a:r0 + SUBLANES * a + span, :]
                z = term if z is None else z + term
        if r == 0:
            acc = acc + z
        else:
            z_ref[r] = z
            acc = acc + z_ref[r, r:r + n, :]
    mu = jnp.mean(acc, axis=-1, keepdims=True)
    xc = acc - mu
    var = jnp.mean(xc * xc, axis=-1, keepdims=True)
    y = xc * lax.rsqrt(var + EPS) * lg_ref[...] + lb_ref[...]
    return y * jax.nn.sigmoid(y)


def _proj_conv_kernel(*refs, nseq, rows, carry):
    if carry:
        h_ref, wv_ref, wg_ref, cw_ref, cb_ref, lg_ref, lb_ref, yc_ref, cn_ref, ext_ref, z_ref = refs
        st_ref = None
    else:
        h_ref, wv_ref, wg_ref, cw_ref, cb_ref, lg_ref, lb_ref, st_ref, yc_ref, cn_ref, ext_ref, z_ref = refs
    hb = h_ref[...]
    u = _dot(hb, wv_ref[...]) * jax.nn.sigmoid(_dot(hb, wg_ref[...]))
    chunk = min(rows, CONV_CHUNK)
    for s in range(nseq):
        if carry:
            @pl.when(pl.program_id(1) == 0)
            def _():
                ext_ref[0:HIST_PAD, :] = jnp.zeros((HIST_PAD, C_CONV), F32)
        else:
            ext_ref[0:HIST_PAD, :] = st_ref[s]
        ext_ref[HIST_PAD:HIST_PAD + rows, :] = u[s * rows:(s + 1) * rows, :]
        for r0 in range(0, rows, chunk):
            y = _conv_rows(ext_ref, z_ref, cw_ref, cb_ref, lg_ref, lb_ref, r0, chunk)
            yc_ref[s * rows + r0:s * rows + r0 + chunk, :] = y.astype(yc_ref.dtype)
        new_hist = ext_ref[rows + HIST_PAD - CONV_HIST:rows + HIST_PAD, :]
        if carry:
            tail = ext_ref[rows:rows + HIST_PAD, :]
            ext_ref[0:HIST_PAD, :] = tail

            @pl.when(pl.program_id(1) == pl.num_programs(1) - 1)
            def _():
                cn_ref[0] = new_hist
        else:
            cn_ref[s] = new_hist


def _proj_conv(h, w_in, layer, conv_w, conv_b, ln_g, ln_b, state_pad, nb, t, tm, y_dtype):
    m, d = h.shape
    carry = state_pad is None
    small = [_col_spec(w_in, layer, C_CONV, 0), _col_spec(w_in, layer, C_CONV, 1),
             _full_spec((CONV_WIDTH, C_CONV)),
             _full_spec((1, C_CONV)), _full_spec((1, C_CONV)), _full_spec((1, C_CONV))]
    if carry:
        assert tm >= HIST_PAD
        nt = t // tm
        grid = (nb, nt)
        in_specs = [pl.BlockSpec((tm, d), lambda b, i: (b * nt + i, 0))] + small
        out_specs = [pl.BlockSpec((tm, C_CONV), lambda b, i: (b * nt + i, 0)),
                     pl.BlockSpec((1, CONV_HIST, C_CONV), lambda b, i: (b, 0, 0))]
        kern = functools.partial(_proj_conv_kernel, nseq=1, rows=tm, carry=True)
        args = (h, w_in, w_in, conv_w, conv_b, ln_g, ln_b)
        rows = tm
    else:
        grid = (1, 1)
        in_specs = [_full_spec((m, d))] + small + [_full_spec((nb, HIST_PAD, C_CONV))]
        out_specs = [_full_spec((m, C_CONV)), _full_spec((nb, CONV_HIST, C_CONV))]
        kern = functools.partial(_proj_conv_kernel, nseq=nb, rows=t, carry=False)
        args = (h, w_in, w_in, conv_w, conv_b, ln_g, ln_b, state_pad)
        rows = t
    return pl.pallas_call(
        kern, name="proj_conv", grid=grid, in_specs=in_specs, out_specs=out_specs,
        out_shape=[jax.ShapeDtypeStruct((m, C_CONV), y_dtype),
                   jax.ShapeDtypeStruct((nb, CONV_HIST, C_CONV), F32)],
        scratch_shapes=[pltpu.VMEM((HIST_PAD + rows, C_CONV), F32),
                        pltpu.VMEM((SUBLANES, min(rows, CONV_CHUNK) + SUBLANES, C_CONV), F32)],
        compiler_params=_cparams(2, 24 << 20),
    )(*args)


def _store_heads(ref, hh, x, nseq, rows):
    for s in range(nseq):
        ref[s, hh] = x[s * rows:(s + 1) * rows, :].astype(ref.dtype)


def _halfnorm_rope(x, g, cos, sin, lane):
    sq = x * x
    lo = lane < DK_DIFF
    s_lo = jnp.sum(jnp.where(lo, sq, 0.0), axis=-1, keepdims=True)
    s_hi = jnp.sum(jnp.where(lo, 0.0, sq), axis=-1, keepdims=True)
    ms = jnp.where(lo, s_lo, s_hi) * (1.0 / DK_DIFF)
    y = x * lax.rsqrt(ms + EPS) * g
    half = DK_DIFF // 2
    rot = jnp.where((lane & (DK_DIFF - 1)) < half,
                    pltpu.roll(y, LANES - half, 1), pltpu.roll(y, half, 1))
    return y * cos + rot * sin


def _proj_diff_kernel(*refs, nseq, rows, n_alias, attn_ops):
    h_ref = refs[0]
    wq_refs, wk_refs, wv_refs = refs[1:1 + N_PAIRS], refs[1 + N_PAIRS:1 + 2 * N_PAIRS], refs[1 + 2 * N_PAIRS:N_W]
    gq_ref, gk_ref, cos_ref, sin_ref = refs[N_W:N_W + 4]
    outs = refs[N_W + 4 + n_alias:]
    q_ref, ks_ref, vs_ref = outs[:3]
    hb = h_ref[...]
    tm = hb.shape[0]
    cos = cos_ref[...]
    sin = sin_ref[...]
    lane = lax.broadcasted_iota(jnp.int32, (tm, LANES), 1)
    scale = DK_DIFF ** -0.5 * (LOG2E if attn_ops else 1.0)
    for pair in range(N_PAIRS):
        zq = _dot(hb, wq_refs[pair][...])
        zk = _dot(hb, wk_refs[pair][...])
        zv = _dot(hb, wv_refs[pair][...])
        for sub in range(2):
            hh = 2 * pair + sub
            sl = slice(sub * HEAD_DIM, (sub + 1) * HEAD_DIM)
            q = _halfnorm_rope(zq[:, sl], gq_ref[...], cos, sin, lane) * scale
            q_ref[:, hh * HEAD_DIM:(hh + 1) * HEAD_DIM] = q.astype(q_ref.dtype)
            k = _halfnorm_rope(zk[:, sl], gk_ref[...], cos, sin, lane)
            v = zv[:, sl]
            _store_heads(ks_ref, hh, k, nseq, rows)
            _store_heads(vs_ref, hh, v, nseq, rows)
            if attn_ops:
                kb_ref, vt_ref = outs[3:5]
                kb_ref[0, hh] = k.astype(BF16)
                vt_ref[0, hh] = v.T.astype(BF16)


def _state_out(stack_shape, layer, nseq, rows):
    spec = pl.BlockSpec((None, nseq, N_HEADS, rows, HEAD_DIM), lambda b, i: (layer, b, 0, i, 0))
    return spec, jax.ShapeDtypeStruct(stack_shape, F32)


def _attn_operand_out(nb, t, tm, kd):
    specs = [pl.BlockSpec((1, N_HEADS, tm, kd), lambda b, i: (b, 0, i, 0)),
             pl.BlockSpec((1, N_HEADS, HEAD_DIM, tm), lambda b, i: (b, 0, 0, i))]
    shapes = [jax.ShapeDtypeStruct((nb, N_HEADS, t, kd), BF16),
              jax.ShapeDtypeStruct((nb, N_HEADS, HEAD_DIM, t), BF16)]
    return specs, shapes


def _alias_args(prev, n_in, first_out):
    specs = [pl.BlockSpec(memory_space=pl.ANY)] * len(prev)
    return list(prev), specs, {n_in + k: first_out + k for k in range(len(prev))}


def _qkv_specs(w_in, layer, first_col):
    base = first_col // PAIR_W
    return [_col_spec(w_in, layer, PAIR_W, base + seg * N_PAIRS + p) for seg in range(3) for p in range(N_PAIRS)]


def _proj_diff(h, w_in, layer, depth, gq, gk, cos, sin, prev, nb, t, tm, is_prompt):
    m, d = h.shape
    if is_prompt:
        nt, grid, nseq, rows = t // tm, (nb, t // tm), 1, tm
    else:
        nt, grid, nseq, rows = 1, (1, 1), nb, t
    row = lambda b, i: (b * nt + i, 0)
    att = pl.BlockSpec((tm, ATT_W), row)
    rope = pl.BlockSpec((tm, LANES), lambda b, i: (i, 0))
    sspec, sshape = _state_out((depth, nb, N_HEADS, t, HEAD_DIM), layer, nseq, rows)
    out_specs = [att, sspec, sspec]
    out_shape = [jax.ShapeDtypeStruct((m, ATT_W), BF16 if is_prompt else F32), sshape, sshape]
    if is_prompt:
        aspecs, ashapes = _attn_operand_out(nb, t, tm, HEAD_DIM)
        out_specs += aspecs
        out_shape += ashapes
    in_specs = ([pl.BlockSpec((tm, d), row)] + _qkv_specs(w_in, layer, DIFF_COL0)
                + [_full_spec((1, LANES)), _full_spec((1, LANES)), rope, rope])
    alias_in, alias_specs, aliases = _alias_args(prev, len(in_specs), 1)
    return pl.pallas_call(
        functools.partial(_proj_diff_kernel, nseq=nseq, rows=rows, n_alias=len(alias_in), attn_ops=is_prompt),
        name="proj_diff", grid=grid, in_specs=in_specs + alias_specs, out_specs=out_specs, out_shape=out_shape,
        input_output_aliases=aliases,
        compiler_params=_cparams(2, 44 << 20),
    )(h, *([w_in] * (3 * N_PAIRS)), gq, gk, cos, sin, *alias_in)


def _scan_lanes(x, seg):
    lane = lax.broadcasted_iota(jnp.int32, x.shape, 1)
    pos = lane & (seg - 1)
    s = 1
    while s < seg:
        x = x + jnp.where(pos >= s, pltpu.roll(x, s, 1), 0.0)
        s *= 2
    return x


def _proj_fox_kernel(*refs, nseq, rows, seg, n_alias, attn_ops):
    h_ref = refs[0]
    wq_refs, wk_refs, wv_refs = refs[1:1 + N_PAIRS], refs[1 + N_PAIRS:1 + 2 * N_PAIRS], refs[1 + 2 * N_PAIRS:N_W]
    wf_ref, fb_ref, gq_ref, gk_ref = refs[N_W:N_W + 4]
    outs = refs[N_W + 4 + n_alias:-1]
    carry_ref = refs[-1]
    q_ref, ks_ref, vs_ref, lf_ref, bias_ref = outs[:5]
    hb = h_ref[...]
    tm = hb.shape[0]
    z = _dot(hb, wf_ref[...]) + fb_ref[...]
    lf = jnp.minimum(z, 0.0) - jnp.log1p(jnp.exp(-jnp.abs(z)))
    if tm < LANES:
        lf = jnp.concatenate([lf, jnp.zeros((LANES - tm, LANES), F32)], axis=0)
    tb = lf.shape[0]
    lft = lf.T[:SUBLANES, :]
    lf_ref[...] = lft
    blocks = []
    if seg >= LANES:
        @pl.when(pl.program_id(1) == 0)
        def _():
            carry_ref[...] = jnp.zeros_like(carry_ref)
        c = carry_ref[...]
        for kb in range(tb // LANES):
            blk = _scan_lanes(lft[:, kb * LANES:(kb + 1) * LANES], LANES) + c
            blocks.append(-blk)
            c = jnp.broadcast_to(blk[:, LANES - 1:LANES], blk.shape)
        carry_ref[...] = c
    else:
        for kb in range(tb // LANES):
            blocks.append(-_scan_lanes(lft[:, kb * LANES:(kb + 1) * LANES], seg))
    bias = jnp.concatenate(blocks, axis=1) if len(blocks) > 1 else blocks[0]
    bias_ref[...] = bias
    if attn_ops:
        bias_t = jnp.concatenate([bias, jnp.zeros((LANES - SUBLANES, tb), F32)], axis=0).T
        lane = lax.broadcasted_iota(jnp.int32, (tm, LANES), 1)

    scale = HEAD_DIM ** -0.5 * (LOG2E if attn_ops else 1.0)
    for pair in range(N_PAIRS):
        zq = _dot(hb, wq_refs[pair][...])
        zk = _dot(hb, wk_refs[pair][...])
        zv = _dot(hb, wv_refs[pair][...])
        for sub in range(2):
            hh = 2 * pair + sub
            sl = slice(sub * HEAD_DIM, (sub + 1) * HEAD_DIM)
            x = zq[:, sl]
            q = x * lax.rsqrt(jnp.mean(x * x, axis=-1, keepdims=True) + EPS) * gq_ref[...] * scale
            q_ref[:, hh * HEAD_DIM:(hh + 1) * HEAD_DIM] = q.astype(q_ref.dtype)
            x = zk[:, sl]
            k = x * lax.rsqrt(jnp.mean(x * x, axis=-1, keepdims=True) + EPS) * gk_ref[...]
            v = zv[:, sl]
            _store_heads(ks_ref, hh, k, nseq, rows)
            _store_heads(vs_ref, hh, v, nseq, rows)
            if attn_ops:
                kb_ref, vt_ref = outs[5:7]
                hi, mid, lo = _split3(jnp.broadcast_to(bias_t[:, hh:hh + 1], (tm, LANES)) * LOG2E)
                aug = jnp.where(lane == 0, hi, jnp.where(lane == 1, mid, jnp.where(lane == 2, lo, 0.0)))
                kb_ref[0, hh, :, 0:HEAD_DIM] = k.astype(BF16)
                kb_ref[0, hh, :, HEAD_DIM:2 * HEAD_DIM] = aug.astype(BF16)
                vt_ref[0, hh] = v.T.astype(BF16)


def _proj_fox(h, w_in, wf, layer, depth, fb, gq, gk, prev, nb, t, tm, is_prompt):
    m, d = h.shape
    if is_prompt:
        nt, grid, nseq, rows, ng = t // tm, (nb, t // tm), 1, tm, nb
    else:
        nt, grid, nseq, rows, ng = 1, (1, 1), nb, t, 1
    tb = max(tm, LANES)
    row = lambda b, i: (b * nt + i, 0)
    att = pl.BlockSpec((tm, ATT_W), row)
    tspec = pl.BlockSpec((None, SUBLANES, tb), lambda b, i: (b, 0, i))
    tshape = jax.ShapeDtypeStruct((ng, SUBLANES, nt * tb), F32)
    sspec, sshape = _state_out((depth, nb, N_HEADS, t, HEAD_DIM), layer, nseq, rows)
    out_specs = [att, sspec, sspec, tspec, tspec]
    out_shape = [jax.ShapeDtypeStruct((m, ATT_W), BF16 if is_prompt else F32), sshape, sshape, tshape, tshape]
    if is_prompt:
        aspecs, ashapes = _attn_operand_out(nb, t, tm, 2 * HEAD_DIM)
        out_specs += aspecs
        out_shape += ashapes
    in_specs = ([pl.BlockSpec((tm, d), row)] + _qkv_specs(w_in, layer, FOX_COL0)
                + [_layer_spec(wf, layer), _full_spec((1, LANES)), _full_spec((1, LANES)), _full_spec((1, LANES))])
    alias_in, alias_specs, aliases = _alias_args(prev, len(in_specs), 1)
    return pl.pallas_call(
        functools.partial(_proj_fox_kernel, nseq=nseq, rows=rows, seg=t, n_alias=len(alias_in),
                          attn_ops=is_prompt),
        name="proj_fox", grid=grid, in_specs=in_specs + alias_specs, out_specs=out_specs, out_shape=out_shape,
        input_output_aliases=aliases,
        scratch_shapes=[pltpu.VMEM((SUBLANES, LANES), F32)],
        compiler_params=_cparams(2, 44 << 20),
    )(h, *([w_in] * (3 * N_PAIRS)), wf, fb, gq, gk, *alias_in)


def _lambda(lq1_ref, lk1_ref, lq2_ref, lk2_ref, lam_init):
    a = jnp.sum(lq1_ref[...] * lk1_ref[...], axis=-1, keepdims=True)
    b = jnp.sum(lq2_ref[...] * lk2_ref[...], axis=-1, keepdims=True)
    return jnp.exp(a) - jnp.exp(b) + lam_init


def _diff_combine(a1, l1, a2, l2, lam, g_sub, lam_init):
    y = a1 / l1 - lam * (a2 / l2)
    y = y * lax.rsqrt(jnp.mean(y * y, axis=-1, keepdims=True) + EPS) * g_sub
    return y * (1.0 - lam_init)


SOFTMAX_BANDS = 2


def _col_reduce(x, reduce_fn, combine_fn):
    band = x.shape[0] // SOFTMAX_BANDS
    parts = [reduce_fn(x[r * band:(r + 1) * band], axis=0, keepdims=True) for r in range(SOFTMAX_BANDS)]
    while len(parts) > 1:
        parts = [combine_fn(parts[2 * r], parts[2 * r + 1]) for r in range(len(parts) // 2)]
    return parts[0]


def _flash_t(i, tk, k_ref, vt_ref, qt, q_of_col, m_ref, acc_ref, sa_ref, sb_ref):
    m_ref[...] = jnp.full(m_ref.shape, NEG_INF, F32)
    acc_ref[...] = jnp.zeros(acc_ref.shape, F32)
    ones = jnp.ones((BF16_ROWS, tk), BF16)

    def scores(j, s_ref):
        start = pl.multiple_of(j * tk, tk)
        s_ref[...] = _dot(k_ref[pl.ds(start, tk), :], qt)

    def soft(j, s_ref, masked):
        start = pl.multiple_of(j * tk, tk)
        s = s_ref[...]
        if masked:
            key = lax.broadcasted_iota(jnp.int32, s.shape, 0)
            s = jnp.where(key <= q_of_col, s, NEG_INF)
        m_old = m_ref[...]
        m_new = jnp.maximum(m_old, _col_reduce(s, jnp.max, jnp.maximum))
        alpha = jnp.exp2(m_old - m_new)
        p = jnp.exp2(s - m_new)
        vt = jnp.concatenate([vt_ref[:, pl.ds(start, tk)], ones], axis=0)
        acc_ref[...] = alpha * acc_ref[...] + _dot(vt, p.astype(BF16))
        m_ref[...] = m_new

    scores(0, sa_ref)

    def body(jj, c):
        j = 2 * jj
        scores(j + 1, sb_ref)
        soft(j, sa_ref, False)
        scores(j + 2, sa_ref)
        soft(j + 1, sb_ref, False)
        return c

    lax.fori_loop(0, lax.shift_right_logical(i, 1), body, 0)
    odd = (i & 1) == 1

    @pl.when(odd)
    def _():
        scores(i, sb_ref)
        soft(i - 1, sa_ref, False)
        soft(i, sb_ref, True)

    @pl.when(jnp.logical_not(odd))
    def _():
        soft(i, sa_ref, True)


def _normalized(acc_ref):
    return acc_ref[0:HEAD_DIM, :] * (1.0 / acc_ref[HEAD_DIM:HEAD_DIM + 1, :])


def _diff_attn_kernel(q_ref, k_ref, vt_ref, lq1_ref, lk1_ref, lq2_ref, lk2_ref, gs_ref, o_ref,
                      m_ref, acc_ref, sa_ref, sb_ref, *, tq, lam_init):
    qt = q_ref[...].astype(F32).T
    sub = lax.broadcasted_iota(jnp.int32, qt.shape, 0)
    qt2 = jnp.concatenate([jnp.where(sub < DK_DIFF, qt, 0.0), jnp.where(sub < DK_DIFF, 0.0, qt)],
                          axis=1).astype(BF16)
    col = lax.broadcasted_iota(jnp.int32, (1, 2 * tq), 1) & (tq - 1)
    _flash_t(pl.program_id(2), tq, k_ref, vt_ref, qt2, col, m_ref, acc_ref, sa_ref, sb_ref)
    lam = _lambda(lq1_ref, lk1_ref, lq2_ref, lk2_ref, lam_init)
    o = _normalized(acc_ref)
    y = (o[:, :tq] - lam * o[:, tq:]).T
    y = y * lax.rsqrt(jnp.mean(y * y, axis=-1, keepdims=True) + EPS) * gs_ref[...]
    o_ref[...] = (y * (1.0 - lam_init)).astype(o_ref.dtype)


def _fox_attn_kernel(q_ref, k_ref, vt_ref, o_ref, m_ref, acc_ref, sa_ref, sb_ref, *, tq):
    qt = q_ref[...].astype(F32).T
    sub = lax.broadcasted_iota(jnp.int32, qt.shape, 0)
    ones = jnp.where(sub < BIAS_PARTS, 1.0, 0.0)
    qa = jnp.concatenate([qt, ones], axis=0).astype(BF16)
    col = lax.broadcasted_iota(jnp.int32, (1, tq), 1)
    _flash_t(pl.program_id(2), tq, k_ref, vt_ref, qa, col, m_ref, acc_ref, sa_ref, sb_ref)
    o_ref[...] = _normalized(acc_ref).T.astype(o_ref.dtype)


def _attn_call(kern, name, q, k, vt, extra, extra_specs, nb, t, tq, ncols):
    assert tq & (tq - 1) == 0 and tq % (SOFTMAX_BANDS * SUBLANES) == 0
    nt = t // tq
    kd = k.shape[-1]
    qspec = pl.BlockSpec((tq, HEAD_DIM), lambda b, h, i: (b * nt + i, h))
    score = pltpu.VMEM((tq, ncols), F32)
    return pl.pallas_call(
        kern, name=name,
        grid=(nb, N_HEADS, nt),
        in_specs=[qspec,
                  pl.BlockSpec((None, None, t, kd), lambda b, h, i: (b, h, 0, 0)),
                  pl.BlockSpec((None, None, HEAD_DIM, t), lambda b, h, i: (b, h, 0, 0))] + extra_specs,
        out_specs=qspec,
        out_shape=jax.ShapeDtypeStruct(q.shape, BF16),
        scratch_shapes=[pltpu.VMEM((1, ncols), F32), pltpu.VMEM((HEAD_DIM + BF16_ROWS, ncols), F32),
                        score, score],
        compiler_params=_cparams(3, 40 << 20),
    )(q, k, vt, *extra)


def _diff_attn(q, k, vt, lq1, lk1, lq2, lk2, g_sub, nb, t, tq, lam_init):
    vec = pl.BlockSpec((1, DK_DIFF), lambda b, h, i: (0, 0))
    return _attn_call(functools.partial(_diff_attn_kernel, tq=tq, lam_init=lam_init), "diff_attn", q, k, vt,
                      [lq1, lk1, lq2, lk2, g_sub],
                      [vec, vec, vec, vec, pl.BlockSpec((1, HEAD_DIM), lambda b, h, i: (0, 0))],
                      nb, t, tq, 2 * tq)


def _fox_attn(q, k, vt, nb, t, tq):
    return _attn_call(functools.partial(_fox_attn_kernel, tq=tq), "fox_attn", q, k, vt, [], [], nb, t, tq, tq)


def _head_rows(q, width, n_groups):
    qt = jnp.concatenate([q] * n_groups, axis=0)
    lane = lax.broadcasted_iota(jnp.int32, qt.shape, 1)
    grp = jnp.right_shift(lax.broadcasted_iota(jnp.int32, qt.shape, 0), 3)
    lo = grp * width
    return jnp.where((lane >= lo) & (lane < lo + width), qt, 0.0).astype(BF16)


def _expand_heads(x, n):
    return jnp.concatenate([jnp.broadcast_to(x[h:h + 1, :], (SUBLANES, n)) for h in range(N_HEADS)], axis=0)


def _flat_heads(ref):
    return jnp.concatenate([ref[h].astype(BF16) for h in range(N_HEADS)], axis=1)


def _dec_update(s, v_tiles, m_ref, l_ref, acc_ref):
    m_old = m_ref[...]
    m_new = jnp.maximum(m_old, jnp.max(s, axis=-1, keepdims=True))
    alpha = jnp.exp(m_old - m_new)
    p = jnp.exp(s - m_new)
    l_ref[...] = alpha * l_ref[...] + jnp.sum(p, axis=-1, keepdims=True)
    pb = p.astype(BF16)
    pv = None
    for r, v in enumerate(v_tiles):
        d = _dot(pb[:, r * LANES:(r + 1) * LANES], v)
        pv = d if pv is None else pv + d
    acc_ref[...] = alpha * acc_ref[...] + pv
    m_ref[...] = m_new


def _new_tokens(qr, kn_ref, vn_ref, bias_rows):
    def padded(ref):
        x = jnp.concatenate([ref[h] for h in range(N_HEADS)], axis=1)
        return jnp.concatenate([x, jnp.zeros((LANES - SUBLANES, ATT_W), F32)], axis=0).astype(BF16)

    s = _dot_nt(qr, padded(kn_ref))
    if bias_rows is not None:
        s = s + bias_rows
    keep = (lax.broadcasted_iota(jnp.int32, s.shape, 1)
            <= (lax.broadcasted_iota(jnp.int32, s.shape, 0) & (SUBLANES - 1)))
    return jnp.where(keep, s, NEG_INF), padded(vn_ref)


def _dec_init(q_ref, qr_ref, m_ref, l_ref, acc_ref, width, n_groups):
    qr_ref[...] = _head_rows(q_ref[...], width, n_groups)
    m_ref[...] = jnp.full_like(m_ref, NEG_INF)
    l_ref[...] = jnp.zeros_like(l_ref)
    acc_ref[...] = jnp.zeros_like(acc_ref)


def _diff_dec_kernel(pt_ref, q_ref, kn_ref, vn_ref, lq1_ref, lk1_ref, lq2_ref, lk2_ref, gs_ref, *rest,
                     npg, lam_init):
    k_refs = rest[:npg]
    v_refs = rest[npg:2 * npg]
    o_ref, qr_ref, m_ref, l_ref, acc_ref = rest[2 * npg:]
    j = pl.program_id(1)

    @pl.when(j == 0)
    def _():
        _dec_init(q_ref, qr_ref, m_ref, l_ref, acc_ref, DK_DIFF, 2 * N_HEADS)

    qr = qr_ref[...]
    s = jnp.concatenate([_dot_nt(qr, _flat_heads(k)) for k in k_refs], axis=1)
    _dec_update(s, [_flat_heads(v) for v in v_refs], m_ref, l_ref, acc_ref)

    @pl.when(j == pl.num_programs(1) - 1)
    def _():
        sn, vn = _new_tokens(qr, kn_ref, vn_ref, None)
        _dec_update(sn, [vn], m_ref, l_ref, acc_ref)
        lam = _lambda(lq1_ref, lk1_ref, lq2_ref, lk2_ref, lam_init)
        acc = acc_ref[...]
        l = l_ref[...]
        for h in range(N_HEADS):
            r1 = slice(2 * h * SUBLANES, (2 * h + 1) * SUBLANES)
            r2 = slice((2 * h + 1) * SUBLANES, (2 * h + 2) * SUBLANES)
            cl = slice(h * HEAD_DIM, (h + 1) * HEAD_DIM)
            o_ref[:, cl] = _diff_combine(acc[r1, cl], l[r1], acc[r2, cl], l[r2], lam, gs_ref[...], lam_init)


def _fox_dec_kernel(pt_ref, q_ref, kn_ref, vn_ref, bn_ref, tri_ref, *rest, npg):
    k_refs = rest[:npg]
    v_refs = rest[npg:2 * npg]
    lf_refs = rest[2 * npg:3 * npg]
    o_ref, qr_ref, m_ref, l_ref, acc_ref, carry_ref = rest[3 * npg:]
    j = pl.program_id(1)

    @pl.when(j == 0)
    def _():
        _dec_init(q_ref, qr_ref, m_ref, l_ref, acc_ref, HEAD_DIM, N_HEADS)
        carry_ref[...] = jnp.zeros_like(carry_ref)

    terms = []
    for lf in lf_refs:
        terms += list(_split3(lf[...]))
    y = _dot(jnp.concatenate(terms, axis=0).astype(BF16), tri_ref[...])
    c = carry_ref[...]
    biases = []
    rows = BIAS_PARTS * SUBLANES
    for r in range(npg):
        yr = y[r * rows:(r + 1) * rows]
        blk = (yr[0:SUBLANES] + yr[SUBLANES:2 * SUBLANES]) + yr[2 * SUBLANES:3 * SUBLANES] + c
        biases.append(-blk)
        c = jnp.broadcast_to(blk[:, LANES - 1:LANES], blk.shape)
    carry_ref[...] = c

    qr = qr_ref[...]
    s = jnp.concatenate([_dot_nt(qr, _flat_heads(k)) for k in k_refs], axis=1)
    s = s + _expand_heads(jnp.concatenate(biases, axis=1), npg * LANES)
    _dec_update(s, [_flat_heads(v) for v in v_refs], m_ref, l_ref, acc_ref)

    @pl.when(j == pl.num_programs(1) - 1)
    def _():
        sn, vn = _new_tokens(qr, kn_ref, vn_ref, _expand_heads(bn_ref[...] - c, LANES))
        _dec_update(sn, [vn], m_ref, l_ref, acc_ref)
        acc = acc_ref[...]
        l = l_ref[...]
        for h in range(N_HEADS):
            rs = slice(h * SUBLANES, (h + 1) * SUBLANES)
            cl = slice(h * HEAD_DIM, (h + 1) * HEAD_DIM)
            o_ref[:, cl] = acc[rs, cl] / l[rs]


def _page_specs(layer, n_pages, npg, block):
    nz = (0,) * (len(block) - 2)

    def spec(r):
        return pl.BlockSpec(block, lambda b, j, pt, r=r: (layer, pt[b * n_pages + j * npg + r]) + nz)
    return [spec(r) for r in range(npg)]


def _dec_common(layer, tnew, rows):
    tok = pl.BlockSpec((tnew, ATT_W), lambda b, j, pt: (b, 0))
    new_kv = pl.BlockSpec((None, None, N_HEADS, tnew, HEAD_DIM), lambda b, j, pt: (layer, b, 0, 0, 0))
    scratch = [pltpu.VMEM((rows, ATT_W), BF16), pltpu.VMEM((rows, 1), F32), pltpu.VMEM((rows, 1), F32),
               pltpu.VMEM((rows, ATT_W), F32)]
    return tok, new_kv, scratch


def _diff_decode(pt, q, kn, vn, lq1, lk1, lq2, lk2, g_sub, cache_k, cache_v, layer, nb, n_pages, lam_init):
    tnew = q.shape[0] // nb
    assert tnew == SUBLANES
    page = cache_k.shape[3]
    npg = _tile(n_pages, DEC_PAGES_PER_STEP)
    tok, new_kv, scratch = _dec_common(layer, tnew, 2 * N_HEADS * SUBLANES)
    vec = pl.BlockSpec((1, DK_DIFF), lambda b, j, pt: (0, 0))
    pages = _page_specs(layer, n_pages, npg, (None, None, N_HEADS, page, HEAD_DIM))
    return pl.pallas_call(
        functools.partial(_diff_dec_kernel, npg=npg, lam_init=lam_init), name="diff_decode",
        grid_spec=pltpu.PrefetchScalarGridSpec(
            num_scalar_prefetch=1, grid=(nb, n_pages // npg),
            in_specs=[tok, new_kv, new_kv, vec, vec, vec, vec,
                      pl.BlockSpec((1, HEAD_DIM), lambda b, j, pt: (0, 0))] + pages + pages,
            out_specs=tok, scratch_shapes=scratch),
        out_shape=jax.ShapeDtypeStruct(q.shape, F32),
        compiler_params=_cparams(2, 4 * npg * math.prod(cache_k.shape[2:]) * 4 + (20 << 20)),
    )(pt, q, kn, vn, lq1, lk1, lq2, lk2, g_sub, *([cache_k] * npg), *([cache_v] * npg))


def _fox_decode(pt, q, kn, vn, bias_new, cache_k, cache_v, cache_lf, layer, nb, n_pages):
    tnew = q.shape[0] // nb
    assert tnew == SUBLANES
    page = cache_k.shape[3]
    assert page == LANES
    npg = _tile(n_pages, DEC_PAGES_PER_STEP)
    tok, new_kv, scratch = _dec_common(layer, tnew, N_HEADS * SUBLANES)
    pages = _page_specs(layer, n_pages, npg, (None, None, N_HEADS, page, HEAD_DIM))
    lf_pages = _page_specs(layer, n_pages, npg, (None, None, SUBLANES, page))
    idx = jnp.arange(page, dtype=jnp.int32)
    tri = (idx[:, None] <= idx[None, :]).astype(BF16)
    return pl.pallas_call(
        functools.partial(_fox_dec_kernel, npg=npg), name="fox_decode",
        grid_spec=pltpu.PrefetchScalarGridSpec(
            num_scalar_prefetch=1, grid=(nb, n_pages // npg),
            in_specs=[tok, new_kv, new_kv,
                      pl.BlockSpec((None, SUBLANES, LANES), lambda b, j, pt: (b, 0, 0)),
                      pl.BlockSpec((page, page), lambda b, j, pt: (0, 0))]
            + pages + pages + lf_pages,
            out_specs=tok, scratch_shapes=scratch + [pltpu.VMEM((SUBLANES, LANES), F32)]),
        out_shape=jax.ShapeDtypeStruct(q.shape, F32),
        compiler_params=_cparams(2, 4 * npg * math.prod(cache_k.shape[2:]) * 4 + (20 << 20)),
    )(pt, q, kn, vn, bias_new, tri, *([cache_k] * npg), *([cache_v] * npg), *([cache_lf] * npg))


def _out_proj_kernel(yc_ref, yd_ref, yf_ref, w_ref, x_ref, g1_ref, gn_ref, sc_ref, sh_ref, x1_ref, h2_ref):
    mix = jnp.concatenate([yc_ref[...].astype(BF16), yd_ref[...].astype(BF16), yf_ref[...].astype(BF16)], axis=1)
    x1 = x_ref[...] + g1_ref[...] * _dot(mix, w_ref[...])
    x1_ref[...] = x1
    h2_ref[...] = _norm_mod_value(x1, gn_ref[...], sc_ref[...], sh_ref[...]).astype(h2_ref.dtype)


def _out_proj(yc, yd, yf, w_out, layer, x, g1, gn, sc2, sh2, tm, tiles_per_group):
    m, d = x.shape
    row = lambda i: (i, 0)
    return pl.pallas_call(
        _out_proj_kernel, name="out_proj",
        grid=(m // tm,),
        in_specs=[pl.BlockSpec((tm, C_CONV), row), pl.BlockSpec((tm, ATT_W), row), pl.BlockSpec((tm, ATT_W), row),
                  _layer_spec(w_out, layer), pl.BlockSpec((tm, d), row), _mod_spec(g1, tiles_per_group),
                  _full_spec((1, d)), _mod_spec(sc2, tiles_per_group), _mod_spec(sh2, tiles_per_group)],
        out_specs=[pl.BlockSpec((tm, d), row), pl.BlockSpec((tm, d), row)],
        out_shape=[jax.ShapeDtypeStruct((m, d), F32), jax.ShapeDtypeStruct((m, d), BF16)],
        compiler_params=_cparams(1, 40 << 20),
    )(yc, yd, yf, w_out, x, g1, gn, sc2, sh2)


def _ffn_up_kernel(h_ref, wa_ref, wb_ref, g_ref):
    hb = h_ref[...]
    a = _dot(hb, wa_ref[...].astype(BF16))
    b = _dot(hb, wb_ref[...].astype(BF16))
    g_ref[...] = (a * jax.nn.sigmoid(a) * b).astype(g_ref.dtype)


def _ffn_up(h2, w_in, layer, tm, tf):
    m, d = h2.shape
    f = w_in.shape[2] // 2
    nf = f // tf
    return pl.pallas_call(
        _ffn_up_kernel, name="ffn_up",
        grid=(m // tm, nf),
        in_specs=[pl.BlockSpec((tm, d), lambda i, j: (i, 0)),
                  pl.BlockSpec((None, d, tf), lambda i, j: (layer, 0, j)),
                  pl.BlockSpec((None, d, tf), lambda i, j: (layer, 0, j + nf))],
        out_specs=pl.BlockSpec((tm, tf), lambda i, j: (i, j)),
        out_shape=jax.ShapeDtypeStruct((m, f), BF16),
        compiler_params=_cparams(2, 40 << 20),
    )(h2, w_in, w_in)


def _ffn_down_kernel(g_ref, w_ref, x_ref, g2_ref, *rest, next_norm):
    x2 = x_ref[...] + g2_ref[...] * _dot(g_ref[...], w_ref[...])
    if next_norm:
        gn_ref, sc_ref, sh_ref, x2_ref, hn_ref = rest
        hn_ref[...] = _norm_mod_value(x2, gn_ref[...], sc_ref[...], sh_ref[...]).astype(hn_ref.dtype)
    else:
        x2_ref, = rest
    x2_ref[...] = x2


def _ffn_down(g, w_out, layer, x1, g2, nxt, tm, tiles_per_group):
    m, f = g.shape
    d = w_out.shape[2]
    xspec = pl.BlockSpec((tm, d), lambda i: (i, 0))
    in_specs = [pl.BlockSpec((tm, f), lambda i: (i, 0)),
                pl.BlockSpec((None, f, d), lambda i: (layer, 0, 0), pipeline_mode=pl.Buffered(1)),
                xspec, _mod_spec(g2, tiles_per_group)]
    args = [g, w_out, x1, g2]
    out_specs, out_shape = [xspec], [jax.ShapeDtypeStruct((m, d), F32)]
    if nxt is not None:
        in_specs += [_full_spec((1, d)), _mod_spec(nxt[1], tiles_per_group), _mod_spec(nxt[2], tiles_per_group)]
        args += list(nxt)
        out_specs.append(xspec)
        out_shape.append(jax.ShapeDtypeStruct((m, d), BF16))
    out = pl.pallas_call(
        functools.partial(_ffn_down_kernel, next_norm=nxt is not None), name="ffn_down",
        grid=(m // tm,),
        in_specs=in_specs, out_specs=out_specs, out_shape=out_shape,
        compiler_params=_cparams(1, f * d * 2 + 6 * tm * (f + 4 * d) + (8 << 20)),
    )(*args)
    return (out[0], out[1]) if nxt is not None else (out[0], None)


def _rope_tables(pos):
    half = DK_DIFF // 2
    inv = ROPE_THETA ** (-jnp.arange(half, dtype=F32) / half)
    ang = pos.astype(F32)[:, None] * inv[None, :]
    cos = jnp.tile(jnp.cos(ang), (1, LANES // half))
    sin = jnp.sin(ang)
    sin = jnp.tile(jnp.concatenate([-sin, sin], axis=1), (1, LANES // DK_DIFF))
    return cos, sin


class _Group:
    def __init__(self, nb, t, is_prompt):
        self.nb, self.t, self.is_prompt = nb, t, is_prompt
        self.m = nb * t
        if is_prompt:
            self.tm = _tile(t, 256)
            self.tm_out = _tile(t, 512)
            self.tm_conv = _tile(t, 512)
            self.tm_ffn = _tile(t, 1024)
            self.tm_down = _tile(t, 256)
            self.tq = _tile(t, 512)
            self.down_tiles = t // self.tm_down
            self.out_tiles = t // self.tm_out
        else:
            self.tm = self.tm_out = self.tm_conv = self.tm_ffn = self.tm_down = self.m
            self.down_tiles = self.out_tiles = 1


def _mods(ada, grp):
    d = ada.shape[1] // 6
    parts = jnp.split(ada, 6, axis=-1)
    if grp.is_prompt:
        return [p.reshape(grp.nb, 1, d) for p in parts]
    return [jnp.repeat(p, grp.t, axis=0).reshape(1, grp.m, d) for p in parts]


def _layer(x, h, grp, wts, lw, layer, depth, mods, nxt, rope, lam_init, states, ctx):
    sh1, sc1, g1, sh2, sc2, g2 = mods
    cos, sin = rope
    nb, t = grp.nb, grp.t
    prompt = grp.is_prompt
    prev_d, prev_f = states[0:2], states[2:4]

    yc, conv_new = _proj_conv(h, wts["w_in"], layer, lw["conv_w"], lw["conv_b"], lw["conv_ln_g"],
                              lw["conv_ln_b"], None if prompt else ctx["state_pad"],
                              nb, t, grp.tm_conv, BF16 if prompt else F32)
    dres = _proj_diff(h, wts["w_in"], layer, depth, lw["diff_qn"], lw["diff_kn"],
                      cos, sin, prev_d, nb, t, grp.tm, prompt)
    fres = _proj_fox(h, wts["w_in"], wts["w_ff"], layer, depth, lw["fox_fb"],
                     lw["fox_qn"], lw["fox_kn"], prev_f, nb, t, grp.tm, prompt)
    qd, kd, vd = dres[:3]
    qf, kf, vf, lft, fbias = fres[:5]
    if prompt:
        yd = _diff_attn(qd, dres[3], dres[4], lw["lam_q1"], lw["lam_k1"], lw["lam_q2"], lw["lam_k2"],
                        lw["diff_subln"], nb, t, grp.tq, lam_init)
        yf = _fox_attn(qf, fres[5], fres[6], nb, t, grp.tq)
    else:
        yd = _diff_decode(ctx["pt"], qd, kd, vd, lw["lam_q1"], lw["lam_k1"], lw["lam_q2"], lw["lam_k2"],
                          lw["diff_subln"], ctx["cache_dk"], ctx["cache_dv"], layer, nb, ctx["n_pages"], lam_init)
        bn = jnp.transpose(fbias[0, :, :nb * t].reshape(SUBLANES, nb, t), (1, 0, 2))
        bn = jnp.pad(bn, ((0, 0), (0, 0), (0, LANES - t)))
        yf = _fox_decode(ctx["pt"], qf, kf, vf, bn, ctx["cache_fk"], ctx["cache_fv"], ctx["cache_lf"],
                         layer, nb, ctx["n_pages"])
    x1, h2 = _out_proj(yc, yd, yf, wts["w_out"], layer, x, g1, lw["norm_ffn"], sc2, sh2, grp.tm_out,
                       grp.out_tiles)
    g = _ffn_up(h2, wts["w_ffn_in"], layer, grp.tm_ffn, wts["tf"])
    x2, h_next = _ffn_down(g, wts["w_ffn_out"], layer, x1, g2, nxt, grp.tm_down, grp.down_tiles)
    return x2, h_next, (kd, vd, kf, vf), lft, conv_new


def kernel(x_prompt, x_sample, cache_diff_k, cache_diff_v, cache_fox_k, cache_fox_v, cache_fox_lf, state_conv,
           page_table, c_prompt, c_sample, norm_mix, norm_ffn, w_ada, b_ada, w_in, conv_w, conv_b, conv_ln_g,
           conv_ln_b, diff_qn, diff_kn, lam_q1, lam_k1, lam_q2, lam_k2, diff_subln, fox_qn, fox_kn, fox_fb,
           w_out, w_ffn_in, w_ffn_out):
    depth = w_in.shape[0]
    bp, tp, d = x_prompt.shape
    bs, ts, _ = x_sample.shape
    page = cache_diff_k.shape[2]
    n_pages = page_table.shape[1]
    d_ff = w_ffn_out.shape[1]
    assert d_ff % LANES == 0
    gp = _Group(bp, tp, True)
    gs = _Group(bs, ts, False)

    c_all = jnp.concatenate([c_prompt, c_sample], axis=0)
    n_c = c_all.shape[0]
    c_all = jnp.pad(c_all, ((0, (-n_c) % (2 * SUBLANES)), (0, 0)))
    ada = _ada(c_all, w_ada, b_ada)
    mods_p = [_mods(ada[l, :bp], gp) for l in range(depth)]
    mods_s = [_mods(ada[l, bp:bp + bs], gs) for l in range(depth)]

    n_main = FOX_COL0 + 3 * ATT_W
    wts = {"w_in": w_in.astype(BF16)}
    wts["w_ff"] = jnp.pad(w_in[:, :, n_main:], ((0, 0), (0, 0), (0, LANES - N_HEADS))).astype(BF16)
    wts["w_out"] = w_out.astype(BF16)
    wts["w_ffn_in"] = w_ffn_in
    wts["w_ffn_out"] = w_ffn_out.astype(BF16)
    wts["tf"] = _tile(d_ff, 512)

    rope_p = _rope_tables(jnp.arange(tp, dtype=jnp.int32))
    cos_s, sin_s = _rope_tables(n_pages * page + jnp.arange(ts, dtype=jnp.int32))
    rope_s = (jnp.tile(cos_s, (bs, 1)), jnp.tile(sin_s, (bs, 1)))

    head_major = lambda c: jnp.transpose(c, (0, 1, 3, 2, 4))
    ctx = dict(pt=page_table.reshape(-1).astype(jnp.int32), n_pages=n_pages,
               cache_dk=head_major(cache_diff_k), cache_dv=head_major(cache_diff_v),
               cache_fk=head_major(cache_fox_k), cache_fv=head_major(cache_fox_v),
               cache_lf=jnp.pad(jnp.transpose(cache_fox_lf, (0, 1, 3, 2)),
                                ((0, 0), (0, 0), (0, SUBLANES - N_HEADS), (0, 0))))
    state_pad = jnp.pad(state_conv, ((0, 0), (0, 0), (HIST_PAD - CONV_HIST, 0), (0, 0)))

    xp = x_prompt.reshape(bp * tp, d)
    xs = x_sample.reshape(bs * ts, d)
    g0 = norm_mix[0][None, :]
    hp = _norm_mod(xp, g0, mods_p[0][1], mods_p[0][0], gp.tm_down, gp.down_tiles)
    hs = _norm_mod(xs, g0, mods_s[0][1], mods_s[0][0], gs.tm, 1)
    kv_p = tuple(jnp.zeros((depth, bp, N_HEADS, tp, HEAD_DIM), F32) for _ in range(4))
    kv_s = tuple(jnp.zeros((depth, bs, N_HEADS, ts, HEAD_DIM), F32) for _ in range(4))
    lf_p, lf_s, conv_p, conv_s = [], [], [], []
    for l in range(depth):
        lam_init = 0.8 - 0.6 * math.exp(-0.3 * l)
        lw = dict(
            conv_w=conv_w[l], conv_b=conv_b[l][None, :], conv_ln_g=conv_ln_g[l][None, :],
            conv_ln_b=conv_ln_b[l][None, :],
            diff_qn=jnp.tile(diff_qn[l], LANES // DK_DIFF)[None, :],
            diff_kn=jnp.tile(diff_kn[l], LANES // DK_DIFF)[None, :],
            lam_q1=lam_q1[l][None, :], lam_k1=lam_k1[l][None, :], lam_q2=lam_q2[l][None, :],
            lam_k2=lam_k2[l][None, :], diff_subln=diff_subln[l][None, :],
            fox_qn=fox_qn[l][None, :], fox_kn=fox_kn[l][None, :],
            fox_fb=jnp.pad(fox_fb[l], (0, LANES - N_HEADS))[None, :],
            norm_ffn=norm_ffn[l][None, :])
        last = l == depth - 1
        nxt_p = None if last else (norm_mix[l + 1][None, :], mods_p[l + 1][1], mods_p[l + 1][0])
        nxt_s = None if last else (norm_mix[l + 1][None, :], mods_s[l + 1][1], mods_s[l + 1][0])
        xp, hp, kv_p, lft, cn = _layer(xp, hp, gp, wts, lw, l, depth, mods_p[l], nxt_p, rope_p, lam_init, kv_p,
                                       None)
        lf_p.append(lft)
        conv_p.append(cn)
        xs, hs, kv_s, lft, cn = _layer(xs, hs, gs, wts, lw, l, depth, mods_s[l], nxt_s, rope_s, lam_init, kv_s,
                                       dict(ctx, state_pad=state_pad[l]))
        lf_s.append(lft)
        conv_s.append(cn)

    kv = lambda a: jnp.transpose(a, (0, 1, 3, 2, 4))
    lf_p = jnp.transpose(jnp.stack(lf_p, axis=0)[:, :, :N_HEADS, :], (0, 1, 3, 2))
    lf_s = jnp.stack(lf_s, axis=0)[:, 0, :N_HEADS, :bs * ts]
    lf_s = jnp.transpose(lf_s.reshape(depth, N_HEADS, bs, ts), (0, 2, 3, 1))
    return (xp.reshape(bp, tp, d), xs.reshape(bs, ts, d),
            kv(kv_p[0]), kv(kv_p[1]), kv(kv_p[2]), kv(kv_p[3]), lf_p, jnp.stack(conv_p, axis=0),
            kv(kv_s[0]), kv(kv_s[1]), kv(kv_s[2]), kv(kv_s[3]), lf_s, jnp.stack(conv_s, axis=0))
```

```python
import functools
import math

import jax
import jax.numpy as jnp
from jax import lax
from jax.experimental import pallas as pl
from jax.experimental.pallas import tpu as pltpu

F32 = jnp.float32
BF16 = jnp.bfloat16

EPS = 1e-6
NEG_INF = -1e30
ROPE_THETA = 10000.0
C_CONV = 512
CONV_WIDTH = 31
CONV_HIST = CONV_WIDTH - 1
N_HEADS = 6
HEAD_DIM = 128
DK_DIFF = 64
ATT_W = N_HEADS * HEAD_DIM
N_PAIRS = N_HEADS // 2
PAIR_W = 2 * HEAD_DIM
N_W = 1 + 3 * N_PAIRS
DIFF_COL0 = 2 * C_CONV
FOX_COL0 = DIFF_COL0 + 3 * ATT_W

LANES = 128
SUBLANES = 8
V7X_VMEM_BYTES = 64 * 1024 * 1024
VMEM_CAP = V7X_VMEM_BYTES - 8 * 1024 * 1024
HIST_PAD = 32
CONV_CHUNK = 64
BIAS_PARTS = 3
LOG2E = 1.4426950408889634
BF16_ROWS = 16

DEC_PAGES_PER_STEP = 16


def _dot(a, b):
    return jnp.dot(a, b, preferred_element_type=F32)


def _dot_nt(a, b):
    return lax.dot_general(a, b, (((1,), (1,)), ((), ())), preferred_element_type=F32)


def _cparams(n_grid, vmem_bytes):
    return pltpu.CompilerParams(dimension_semantics=("arbitrary",) * n_grid,
                                vmem_limit_bytes=int(min(max(vmem_bytes, 16 << 20), VMEM_CAP)))


def _tile(n, pref):
    t = min(n, pref)
    assert n % t == 0, (n, pref)
    return t


def _full_spec(shape):
    nd = len(shape)
    return pl.BlockSpec(shape, lambda *_: (0,) * nd)


def _layer_spec(w, layer):
    return pl.BlockSpec((None,) + tuple(w.shape[1:]), lambda *_: (layer, 0, 0), pipeline_mode=pl.Buffered(1))


def _col_spec(w, layer, width, block):
    return pl.BlockSpec((None, w.shape[1], width), lambda *_: (layer, 0, block), pipeline_mode=pl.Buffered(1))


def _mod_spec(mod, tiles_per_group):
    _, r, d = mod.shape
    return pl.BlockSpec((None, r, d), lambda i, *_: (i // tiles_per_group, 0, 0))


def _split3(x):
    hi = x.astype(BF16).astype(F32)
    r1 = x - hi
    mid = r1.astype(BF16).astype(F32)
    lo = (r1 - mid).astype(BF16).astype(F32)
    return hi, mid, lo


def _ada_kernel(c_ref, w_ref, b_ref, o_ref):
    c = c_ref[...]
    a = c * jax.nn.sigmoid(c)
    a_hi = a.astype(BF16)
    a_lo = (a - a_hi.astype(F32)).astype(BF16)
    w = w_ref[...].astype(BF16)
    o_ref[...] = _dot(a_hi, w) + _dot(a_lo, w) + b_ref[...]


def _ada(c_all, w_ada, b_ada):
    depth, d, n = w_ada.shape
    rows = c_all.shape[0]
    tn = _tile(n, 512)
    return pl.pallas_call(
        _ada_kernel, name="ada",
        grid=(depth, n // tn),
        in_specs=[pl.BlockSpec((rows, d), lambda l, j: (0, 0)),
                  pl.BlockSpec((None, d, tn), lambda l, j: (l, 0, j)),
                  pl.BlockSpec((None, 1, tn), lambda l, j: (l, 0, j))],
        out_specs=pl.BlockSpec((None, rows, tn), lambda l, j: (l, 0, j)),
        out_shape=jax.ShapeDtypeStruct((depth, rows, n), F32),
        compiler_params=_cparams(2, 6 * d * tn * 4),
    )(c_all, w_ada, b_ada.reshape(depth, 1, n))


def _norm_mod_value(x, g, sc, sh):
    y = x * lax.rsqrt(jnp.mean(x * x, axis=-1, keepdims=True) + EPS) * g
    return y * (1.0 + sc) + sh


def _norm_mod_kernel(x_ref, g_ref, sc_ref, sh_ref, h_ref):
    h_ref[...] = _norm_mod_value(x_ref[...], g_ref[...], sc_ref[...], sh_ref[...]).astype(h_ref.dtype)


def _norm_mod(x, g, sc, sh, tm, tiles_per_group):
    m, d = x.shape
    return pl.pallas_call(
        _norm_mod_kernel, name="norm_mod",
        grid=(m // tm,),
        in_specs=[pl.BlockSpec((tm, d), lambda i: (i, 0)), _full_spec((1, d)),
                  _mod_spec(sc, tiles_per_group), _mod_spec(sh, tiles_per_group)],
        out_specs=pl.BlockSpec((tm, d), lambda i: (i, 0)),
        out_shape=jax.ShapeDtypeStruct((m, d), BF16),
        compiler_params=_cparams(1, 8 * tm * d * 4),
    )(x, g, sc, sh)


def _conv_rows(ext_ref, z_ref, cw_ref, cb_ref, lg_ref, lb_ref, r0, n):
    acc = jnp.broadcast_to(cb_ref[...], (n, C_CONV))
    off = HIST_PAD - CONV_HIST
    for r in range(SUBLANES):
        span = n if r == 0 else n + SUBLANES
        z = None
        for a in range((CONV_WIDTH + off) // SUBLANES + 1):
            j = SUBLANES * a + r - off
            if 0 <= j < CONV_WIDTH:
                term = cw_ref[j:j + 1, :] * ext_ref[r0 + SUBLANES * a:r0 + SUBLANES * a + span, :]
                z = term if z is None else z + term
        if r == 0:
            acc = acc + z
        else:
            z_ref[r] = z
            acc = acc + z_ref[r, r:r + n, :]
    mu = jnp.mean(acc, axis=-1, keepdims=True)
    xc = acc - mu
    var = jnp.mean(xc * xc, axis=-1, keepdims=True)
    y = xc * lax.rsqrt(var + EPS) * lg_ref[...] + lb_ref[...]
    return y * jax.nn.sigmoid(y)


def _proj_conv_kernel(*refs, nseq, rows, carry):
    if carry:
        h_ref, wv_ref, wg_ref, cw_ref, cb_ref, lg_ref, lb_ref, yc_ref, cn_ref, ext_ref, z_ref = refs
        st_ref = None
    else:
        h_ref, wv_ref, wg_ref, cw_ref, cb_ref, lg_ref, lb_ref, st_ref, yc_ref, cn_ref, ext_ref, z_ref = refs
    hb = h_ref[...]
    u = _dot(hb, wv_ref[...]) * jax.nn.sigmoid(_dot(hb, wg_ref[...]))
    chunk = min(rows, CONV_CHUNK)
    for s in range(nseq):
        if carry:
            @pl.when(pl.program_id(1) == 0)
            def _():
                ext_ref[0:HIST_PAD, :] = jnp.zeros((HIST_PAD, C_CONV), F32)
        else:
            ext_ref[0:HIST_PAD, :] = st_ref[s]
        ext_ref[HIST_PAD:HIST_PAD + rows, :] = u[s * rows:(s + 1) * rows, :]
        for r0 in range(0, rows, chunk):
            y = _conv_rows(ext_ref, z_ref, cw_ref, cb_ref, lg_ref, lb_ref, r0, chunk)
            yc_ref[s * rows + r0:s * rows + r0 + chunk, :] = y.astype(yc_ref.dtype)
        new_hist = ext_ref[rows + HIST_PAD - CONV_HIST:rows + HIST_PAD, :]
        if carry:
            tail = ext_ref[rows:rows + HIST_PAD, :]
            ext_ref[0:HIST_PAD, :] = tail

            @pl.when(pl.program_id(1) == pl.num_programs(1) - 1)
            def _():
                cn_ref[0] = new_hist
        else:
            cn_ref[s] = new_hist


def _proj_conv(h, w_in, layer, conv_w, conv_b, ln_g, ln_b, state_pad, nb, t, tm, y_dtype):
    m, d = h.shape
    carry = state_pad is None
    small = [_col_spec(w_in, layer, C_CONV, 0), _col_spec(w_in, layer, C_CONV, 1),
             _full_spec((CONV_WIDTH, C_CONV)),
             _full_spec((1, C_CONV)), _full_spec((1, C_CONV)), _full_spec((1, C_CONV))]
    if carry:
        assert tm >= HIST_PAD
        nt = t // tm
        grid = (nb, nt)
        in_specs = [pl.BlockSpec((tm, d), lambda b, i: (b * nt + i, 0))] + small
        out_specs = [pl.BlockSpec((tm, C_CONV), lambda b, i: (b * nt + i, 0)),
                     pl.BlockSpec((1, CONV_HIST, C_CONV), lambda b, i: (b, 0, 0))]
        kern = functools.partial(_proj_conv_kernel, nseq=1, rows=tm, carry=True)
        args = (h, w_in, w_in, conv_w, conv_b, ln_g, ln_b)
        rows = tm
    else:
        grid = (1, 1)
        in_specs = [_full_spec((m, d))] + small + [_full_spec((nb, HIST_PAD, C_CONV))]
        out_specs = [_full_spec((m, C_CONV)), _full_spec((nb, CONV_HIST, C_CONV))]
        kern = functools.partial(_proj_conv_kernel, nseq=nb, rows=t, carry=False)
        args = (h, w_in, w_in, conv_w, conv_b, ln_g, ln_b, state_pad)
        rows = t
    return pl.pallas_call(
        kern, name="proj_conv", grid=grid, in_specs=in_specs, out_specs=out_specs,
        out_shape=[jax.ShapeDtypeStruct((m, C_CONV), y_dtype),
                   jax.ShapeDtypeStruct((nb, CONV_HIST, C_CONV), F32)],
        scratch_shapes=[pltpu.VMEM((HIST_PAD + rows, C_CONV), F32),
                        pltpu.VMEM((SUBLANES, min(rows, CONV_CHUNK) + SUBLANES, C_CONV), F32)],
        compiler_params=_cparams(2, 24 << 20),
    )(*args)


def _store_heads(ref, hh, x, nseq, rows):
    for s in range(nseq):
        ref[s, hh] = x[s * rows:(s + 1) * rows, :].astype(ref.dtype)


def _halfnorm_rope(x, g, cos, sin, lane):
    sq = x * x
    lo = lane < DK_DIFF
    s_lo = jnp.sum(jnp.where(lo, sq, 0.0), axis=-1, keepdims=True)
    s_hi = jnp.sum(jnp.where(lo, 0.0, sq), axis=-1, keepdims=True)
    ms = jnp.where(lo, s_lo, s_hi) * (1.0 / DK_DIFF)
    y = x * lax.rsqrt(ms + EPS) * g
    half = DK_DIFF // 2
    rot = jnp.where((lane & (DK_DIFF - 1)) < half,
                    pltpu.roll(y, LANES - half, 1), pltpu.roll(y, half, 1))
    return y * cos + rot * sin


def _proj_diff_kernel(*refs, nseq, rows, n_alias, attn_ops):
    h_ref = refs[0]
    wq_refs, wk_refs, wv_refs = refs[1:1 + N_PAIRS], refs[1 + N_PAIRS:1 + 2 * N_PAIRS], refs[1 + 2 * N_PAIRS:N_W]
    gq_ref, gk_ref, cos_ref, sin_ref = refs[N_W:N_W + 4]
    outs = refs[N_W + 4 + n_alias:]
    q_ref, ks_ref, vs_ref = outs[:3]
    hb = h_ref[...]
    tm = hb.shape[0]
    cos = cos_ref[...]
    sin = sin_ref[...]
    lane = lax.broadcasted_iota(jnp.int32, (tm, LANES), 1)
    scale = DK_DIFF ** -0.5 * (LOG2E if attn_ops else 1.0)
    for pair in range(N_PAIRS):
        zq = _dot(hb, wq_refs[pair][...])
        zk = _dot(hb, wk_refs[pair][...])
        zv = _dot(hb, wv_refs[pair][...])
        for sub in range(2):
            hh = 2 * pair + sub
            sl = slice(sub * HEAD_DIM, (sub + 1) * HEAD_DIM)
            q = _halfnorm_rope(zq[:, sl], gq_ref[...], cos, sin, lane) * scale
            q_ref[:, hh * HEAD_DIM:(hh + 1) * HEAD_DIM] = q.astype(q_ref.dtype)
            k = _halfnorm_rope(zk[:, sl], gk_ref[...], cos, sin, lane)
            v = zv[:, sl]
            _store_heads(ks_ref, hh, k, nseq, rows)
            _store_heads(vs_ref, hh, v, nseq, rows)
            if attn_ops:
                kb_ref, vt_ref = outs[3:5]
                kb_ref[0, hh] = k.astype(BF16)
                vt_ref[0, hh] = v.T.astype(BF16)


def _state_out(stack_shape, layer, nseq, rows):
    spec = pl.BlockSpec((None, nseq, N_HEADS, rows, HEAD_DIM), lambda b, i: (layer, b, 0, i, 0))
    return spec, jax.ShapeDtypeStruct(stack_shape, F32)


def _attn_operand_out(nb, t, tm, kd):
    specs = [pl.BlockSpec((1, N_HEADS, tm, kd), lambda b, i: (b, 0, i, 0)),
             pl.BlockSpec((1, N_HEADS, HEAD_DIM, tm), lambda b, i: (b, 0, 0, i))]
    shapes = [jax.ShapeDtypeStruct((nb, N_HEADS, t, kd), BF16),
              jax.ShapeDtypeStruct((nb, N_HEADS, HEAD_DIM, t), BF16)]
    return specs, shapes


def _alias_args(prev, n_in, first_out):
    specs = [pl.BlockSpec(memory_space=pl.ANY)] * len(prev)
    return list(prev), specs, {n_in + k: first_out + k for k in range(len(prev))}


def _qkv_specs(w_in, layer, first_col):
    base = first_col // PAIR_W
    return [_col_spec(w_in, layer, PAIR_W, base + seg * N_PAIRS + p) for seg in range(3) for p in range(N_PAIRS)]


def _proj_diff(h, w_in, layer, depth, gq, gk, cos, sin, prev, nb, t, tm, is_prompt):
    m, d = h.shape
    if is_prompt:
        nt, grid, nseq, rows = t // tm, (nb, t // tm), 1, tm
    else:
        nt, grid, nseq, rows = 1, (1, 1), nb, t
    row = lambda b, i: (b * nt + i, 0)
    att = pl.BlockSpec((tm, ATT_W), row)
    rope = pl.BlockSpec((tm, LANES), lambda b, i: (i, 0))
    sspec, sshape = _state_out((depth, nb, N_HEADS, t, HEAD_DIM), layer, nseq, rows)
    out_specs = [att, sspec, sspec]
    out_shape = [jax.ShapeDtypeStruct((m, ATT_W), BF16 if is_prompt else F32), sshape, sshape]
    if is_prompt:
        aspecs, ashapes = _attn_operand_out(nb, t, tm, HEAD_DIM)
        out_specs += aspecs
        out_shape += ashapes
    in_specs = ([pl.BlockSpec((tm, d), row)] + _qkv_specs(w_in, layer, DIFF_COL0)
                + [_full_spec((1, LANES)), _full_spec((1, LANES)), rope, rope])
    alias_in, alias_specs, aliases = _alias_args(prev, len(in_specs), 1)
    return pl.pallas_call(
        functools.partial(_proj_diff_kernel, nseq=nseq, rows=rows, n_alias=len(alias_in), attn_ops=is_prompt),
        name="proj_diff", grid=grid, in_specs=in_specs + alias_specs, out_specs=out_specs, out_shape=out_shape,
        input_output_aliases=aliases,
        compiler_params=_cparams(2, 44 << 20),
    )(h, *([w_in] * (3 * N_PAIRS)), gq, gk, cos, sin, *alias_in)


def _scan_lanes(x, seg):
    lane = lax.broadcasted_iota(jnp.int32, x.shape, 1)
    pos = lane & (seg - 1)
    s = 1
    while s < seg:
        x = x + jnp.where(pos >= s, pltpu.roll(x, s, 1), 0.0)
        s *= 2
    return x


def _proj_fox_kernel(*refs, nseq, rows, seg, n_alias, attn_ops):
    h_ref = refs[0]
    wq_refs, wk_refs, wv_refs = refs[1:1 + N_PAIRS], refs[1 + N_PAIRS:1 + 2 * N_PAIRS], refs[1 + 2 * N_PAIRS:N_W]
    wf_ref, fb_ref, gq_ref, gk_ref = refs[N_W:N_W + 4]
    outs = refs[N_W + 4 + n_alias:-1]
    carry_ref = refs[-1]
    q_ref, ks_ref, vs_ref, lf_ref, bias_ref = outs[:5]
    hb = h_ref[...]
    tm = hb.shape[0]
    z = _dot(hb, wf_ref[...]) + fb_ref[...]
    lf = jnp.minimum(z, 0.0) - jnp.log1p(jnp.exp(-jnp.abs(z)))
    if tm < LANES:
        lf = jnp.concatenate([lf, jnp.zeros((LANES - tm, LANES), F32)], axis=0)
    tb = lf.shape[0]
    lft = lf.T[:SUBLANES, :]
    lf_ref[...] = lft
    blocks = []
    if seg >= LANES:
        @pl.when(pl.program_id(1) == 0)
        def _():
            carry_ref[...] = jnp.zeros_like(carry_ref)
        c = carry_ref[...]
        for kb in range(tb // LANES):
            blk = _scan_lanes(lft[:, kb * LANES:(kb + 1) * LANES], LANES) + c
            blocks.append(-blk)
            c = jnp.broadcast_to(blk[:, LANES - 1:LANES], blk.shape)
        carry_ref[...] = c
    else:
        for kb in range(tb // LANES):
            blocks.append(-_scan_lanes(lft[:, kb * LANES:(kb + 1) * LANES], seg))
    bias = jnp.concatenate(blocks, axis=1) if len(blocks) > 1 else blocks[0]
    bias_ref[...] = bias
    if attn_ops:
        bias_t = jnp.concatenate([bias, jnp.zeros((LANES - SUBLANES, tb), F32)], axis=0).T
        lane = lax.broadcasted_iota(jnp.int32, (tm, LANES), 1)

    scale = HEAD_DIM ** -0.5 * (LOG2E if attn_ops else 1.0)
    for pair in range(N_PAIRS):
        zq = _dot(hb, wq_refs[pair][...])
        zk = _dot(hb, wk_refs[pair][...])
        zv = _dot(hb, wv_refs[pair][...])
        for sub in range(2):
            hh = 2 * pair + sub
            sl = slice(sub * HEAD_DIM, (sub + 1) * HEAD_DIM)
            x = zq[:, sl]
            q = x * lax.rsqrt(jnp.mean(x * x, axis=-1, keepdims=True) + EPS) * gq_ref[...] * scale
            q_ref[:, hh * HEAD_DIM:(hh + 1) * HEAD_DIM] = q.astype(q_ref.dtype)
            x = zk[:, sl]
            k = x * lax.rsqrt(jnp.mean(x * x, axis=-1, keepdims=True) + EPS) * gk_ref[...]
            v = zv[:, sl]
            _store_heads(ks_ref, hh, k, nseq, rows)
            _store_heads(vs_ref, hh, v, nseq, rows)
            if attn_ops:
                kb_ref, vt_ref = outs[5:7]
                hi, mid, lo = _split3(jnp.broadcast_to(bias_t[:, hh:hh + 1], (tm, LANES)) * LOG2E)
                aug = jnp.where(lane == 0, hi, jnp.where(lane == 1, mid, jnp.where(lane == 2, lo, 0.0)))
                kb_ref[0, hh, :, 0:HEAD_DIM] = k.astype(BF16)
                kb_ref[0, hh, :, HEAD_DIM:2 * HEAD_DIM] = aug.astype(BF16)
                vt_ref[0, hh] = v.T.astype(BF16)


def _proj_fox(h, w_in, wf, layer, depth, fb, gq, gk, prev, nb, t, tm, is_prompt):
    m, d = h.shape
    if is_prompt:
        nt, grid, nseq, rows, ng = t // tm, (nb, t // tm), 1, tm, nb
    else:
        nt, grid, nseq, rows, ng = 1, (1, 1), nb, t, 1
    tb = max(tm, LANES)
    row = lambda b, i: (b * nt + i, 0)
    att = pl.BlockSpec((tm, ATT_W), row)
    tspec = pl.BlockSpec((None, SUBLANES, tb), lambda b, i: (b, 0, i))
    tshape = jax.ShapeDtypeStruct((ng, SUBLANES, nt * tb), F32)
    sspec, sshape = _state_out((depth, nb, N_HEADS, t, HEAD_DIM), layer, nseq, rows)
    out_specs = [att, sspec, sspec, tspec, tspec]
    out_shape = [jax.ShapeDtypeStruct((m, ATT_W), BF16 if is_prompt else F32), sshape, sshape, tshape, tshape]
    if is_prompt:
        aspecs, ashapes = _attn_operand_out(nb, t, tm, 2 * HEAD_DIM)
        out_specs += aspecs
        out_shape += ashapes
    in_specs = ([pl.BlockSpec((tm, d), row)] + _qkv_specs(w_in, layer, FOX_COL0)
                + [_layer_spec(wf, layer), _full_spec((1, LANES)), _full_spec((1, LANES)), _full_spec((1, LANES))])
    alias_in, alias_specs, aliases = _alias_args(prev, len(in_specs), 1)
    return pl.pallas_call(
        functools.partial(_proj_fox_kernel, nseq=nseq, rows=rows, seg=t, n_alias=len(alias_in),
                          attn_ops=is_prompt),
        name="proj_fox", grid=grid, in_specs=in_specs + alias_specs, out_specs=out_specs, out_shape=out_shape,
        input_output_aliases=aliases,
        scratch_shapes=[pltpu.VMEM((SUBLANES, LANES), F32)],
        compiler_params=_cparams(2, 44 << 20),
    )(h, *([w_in] * (3 * N_PAIRS)), wf, fb, gq, gk, *alias_in)


def _lambda(lq1_ref, lk1_ref, lq2_ref, lk2_ref, lam_init):
    a = jnp.sum(lq1_ref[...] * lk1_ref[...], axis=-1, keepdims=True)
    b = jnp.sum(lq2_ref[...] * lk2_ref[...], axis=-1, keepdims=True)
    return jnp.exp(a) - jnp.exp(b) + lam_init


def _diff_combine(a1, l1, a2, l2, lam, g_sub, lam_init):
    y = a1 / l1 - lam * (a2 / l2)
    y = y * lax.rsqrt(jnp.mean(y * y, axis=-1, keepdims=True) + EPS) * g_sub
    return y * (1.0 - lam_init)


SOFTMAX_BANDS = 2


def _col_reduce(x, reduce_fn, combine_fn):
    band = x.shape[0] // SOFTMAX_BANDS
    parts = [reduce_fn(x[r * band:(r + 1) * band], axis=0, keepdims=True) for r in range(SOFTMAX_BANDS)]
    while len(parts) > 1:
        parts = [combine_fn(parts[2 * r], parts[2 * r + 1]) for r in range(len(parts) // 2)]
    return parts[0]


def _flash_t(i, tk, k_ref, vt_ref, qt, q_of_col, m_ref, acc_ref, sa_ref, sb_ref):
    m_ref[...] = jnp.full(m_ref.shape, NEG_INF, F32)
    acc_ref[...] = jnp.zeros(acc_ref.shape, F32)
    ones = jnp.ones((BF16_ROWS, tk), BF16)

    def scores(j, s_ref):
        start = pl.multiple_of(j * tk, tk)
        s_ref[...] = _dot(k_ref[pl.ds(start, tk), :], qt)

    def soft(j, s_ref, masked):
        start = pl.multiple_of(j * tk, tk)
        s = s_ref[...]
        if masked:
            key = lax.broadcasted_iota(jnp.int32, s.shape, 0)
            s = jnp.where(key <= q_of_col, s, NEG_INF)
        m_old = m_ref[...]
        m_new = jnp.maximum(m_old, _col_reduce(s, jnp.max, jnp.maximum))
        alpha = jnp.exp2(m_old - m_new)
        p = jnp.exp2(s - m_new)
        vt = jnp.concatenate([vt_ref[:, pl.ds(start, tk)], ones], axis=0)
        acc_ref[...] = alpha * acc_ref[...] + _dot(vt, p.astype(BF16))
        m_ref[...] = m_new

    scores(0, sa_ref)

    def body(jj, c):
        j = 2 * jj
        scores(j + 1, sb_ref)
        soft(j, sa_ref, False)
        scores(j + 2, sa_ref)
        soft(j + 1, sb_ref, False)
        return c

    lax.fori_loop(0, lax.shift_right_logical(i, 1), body, 0)
    odd = (i & 1) == 1

    @pl.when(odd)
    def _():
        scores(i, sb_ref)
        soft(i - 1, sa_ref, False)
        soft(i, sb_ref, True)

    @pl.when(jnp.logical_not(odd))
    def _():
        soft(i, sa_ref, True)


def _normalized(acc_ref):
    return acc_ref[0:HEAD_DIM, :] * (1.0 / acc_ref[HEAD_DIM:HEAD_DIM + 1, :])


def _diff_attn_kernel(q_ref, k_ref, vt_ref, lq1_ref, lk1_ref, lq2_ref, lk2_ref, gs_ref, o_ref,
                      m_ref, acc_ref, sa_ref, sb_ref, *, tq, lam_init):
    qt = q_ref[...].astype(F32).T
    sub = lax.broadcasted_iota(jnp.int32, qt.shape, 0)
    qt2 = jnp.concatenate([jnp.where(sub < DK_DIFF, qt, 0.0), jnp.where(sub < DK_DIFF, 0.0, qt)],
                          axis=1).astype(BF16)
    col = lax.broadcasted_iota(jnp.int32, (1, 2 * tq), 1) & (tq - 1)
    _flash_t(pl.program_id(2), tq, k_ref, vt_ref, qt2, col, m_ref, acc_ref, sa_ref, sb_ref)
    lam = _lambda(lq1_ref, lk1_ref, lq2_ref, lk2_ref, lam_init)
    o = _normalized(acc_ref)
    y = (o[:, :tq] - lam * o[:, tq:]).T
    y = y * lax.rsqrt(jnp.mean(y * y, axis=-1, keepdims=True) + EPS) * gs_ref[...]
    o_ref[...] = (y * (1.0 - lam_init)).astype(o_ref.dtype)


def _fox_attn_kernel(q_ref, k_ref, vt_ref, o_ref, m_ref, acc_ref, sa_ref, sb_ref, *, tq):
    qt = q_ref[...].astype(F32).T
    sub = lax.broadcasted_iota(jnp.int32, qt.shape, 0)
    ones = jnp.where(sub < BIAS_PARTS, 1.0, 0.0)
    qa = jnp.concatenate([qt, ones], axis=0).astype(BF16)
    col = lax.broadcasted_iota(jnp.int32, (1, tq), 1)
    _flash_t(pl.program_id(2), tq, k_ref, vt_ref, qa, col, m_ref, acc_ref, sa_ref, sb_ref)
    o_ref[...] = _normalized(acc_ref).T.astype(o_ref.dtype)


def _attn_call(kern, name, q, k, vt, extra, extra_specs, nb, t, tq, ncols):
    assert tq & (tq - 1) == 0 and tq % (SOFTMAX_BANDS * SUBLANES) == 0
    nt = t // tq
    kd = k.shape[-1]
    qspec = pl.BlockSpec((tq, HEAD_DIM), lambda b, h, i: (b * nt + i, h))
    score = pltpu.VMEM((tq, ncols), F32)
    return pl.pallas_call(
        kern, name=name,
        grid=(nb, N_HEADS, nt),
        in_specs=[qspec,
                  pl.BlockSpec((None, None, t, kd), lambda b, h, i: (b, h, 0, 0)),
                  pl.BlockSpec((None, None, HEAD_DIM, t), lambda b, h, i: (b, h, 0, 0))] + extra_specs,
        out_specs=qspec,
        out_shape=jax.ShapeDtypeStruct(q.shape, BF16),
        scratch_shapes=[pltpu.VMEM((1, ncols), F32), pltpu.VMEM((HEAD_DIM + BF16_ROWS, ncols), F32),
                        score, score],
        compiler_params=_cparams(3, 40 << 20),
    )(q, k, vt, *extra)


def _diff_attn(q, k, vt, lq1, lk1, lq2, lk2, g_sub, nb, t, tq, lam_init):
    vec = pl.BlockSpec((1, DK_DIFF), lambda b, h, i: (0, 0))
    return _attn_call(functools.partial(_diff_attn_kernel, tq=tq, lam_init=lam_init), "diff_attn", q, k, vt,
                      [lq1, lk1, lq2, lk2, g_sub],
                      [vec, vec, vec, vec, pl.BlockSpec((1, HEAD_DIM), lambda b, h, i: (0, 0))],
                      nb, t, tq, 2 * tq)


def _fox_attn(q, k, vt, nb, t, tq):
    return _attn_call(functools.partial(_fox_attn_kernel, tq=tq), "fox_attn", q, k, vt, [], [], nb, t, tq, tq)


def _head_rows(q, width, n_groups):
    qt = jnp.concatenate([q] * n_groups, axis=0)
    lane = lax.broadcasted_iota(jnp.int32, qt.shape, 1)
    grp = jnp.right_shift(lax.broadcasted_iota(jnp.int32, qt.shape, 0), 3)
    lo = grp * width
    return jnp.where((lane >= lo) & (lane < lo + width), qt, 0.0).astype(BF16)


def _expand_heads(x, n):
    return jnp.concatenate([jnp.broadcast_to(x[h:h + 1, :], (SUBLANES, n)) for h in range(N_HEADS)], axis=0)


def _flat_heads(ref):
    return jnp.concatenate([ref[h].astype(BF16) for h in range(N_HEADS)], axis=1)


def _dec_update(s, v_tiles, m_ref, l_ref, acc_ref):
    m_old = m_ref[...]
    m_new = jnp.maximum(m_old, jnp.max(s, axis=-1, keepdims=True))
    alpha = jnp.exp(m_old - m_new)
    p = jnp.exp(s - m_new)
    l_ref[...] = alpha * l_ref[...] + jnp.sum(p, axis=-1, keepdims=True)
    pb = p.astype(BF16)
    pv = None
    for r, v in enumerate(v_tiles):
        d = _dot(pb[:, r * LANES:(r + 1) * LANES], v)
        pv = d if pv is None else pv + d
    acc_ref[...] = alpha * acc_ref[...] + pv
    m_ref[...] = m_new


def _new_tokens(qr, kn_ref, vn_ref, bias_rows):
    def padded(ref):
        x = jnp.concatenate([ref[h] for h in range(N_HEADS)], axis=1)
        return jnp.concatenate([x, jnp.zeros((LANES - SUBLANES, ATT_W), F32)], axis=0).astype(BF16)

    s = _dot_nt(qr, padded(kn_ref))
    if bias_rows is not None:
        s = s + bias_rows
    keep = (lax.broadcasted_iota(jnp.int32, s.shape, 1)
            <= (lax.broadcasted_iota(jnp.int32, s.shape, 0) & (SUBLANES - 1)))
    return jnp.where(keep, s, NEG_INF), padded(vn_ref)


def _dec_init(q_ref, qr_ref, m_ref, l_ref, acc_ref, width, n_groups):
    qr_ref[...] = _head_rows(q_ref[...], width, n_groups)
    m_ref[...] = jnp.full_like(m_ref, NEG_INF)
    l_ref[...] = jnp.zeros_like(l_ref)
    acc_ref[...] = jnp.zeros_like(acc_ref)


def _diff_dec_kernel(pt_ref, q_ref, kn_ref, vn_ref, lq1_ref, lk1_ref, lq2_ref, lk2_ref, gs_ref, *rest,
                     npg, lam_init):
    k_refs = rest[:npg]
    v_refs = rest[npg:2 * npg]
    o_ref, qr_ref, m_ref, l_ref, acc_ref = rest[2 * npg:]
    j = pl.program_id(1)

    @pl.when(j == 0)
    def _():
        _dec_init(q_ref, qr_ref, m_ref, l_ref, acc_ref, DK_DIFF, 2 * N_HEADS)

    qr = qr_ref[...]
    s = jnp.concatenate([_dot_nt(qr, _flat_heads(k)) for k in k_refs], axis=1)
    _dec_update(s, [_flat_heads(v) for v in v_refs], m_ref, l_ref, acc_ref)

    @pl.when(j == pl.num_programs(1) - 1)
    def _():
        sn, vn = _new_tokens(qr, kn_ref, vn_ref, None)
        _dec_update(sn, [vn], m_ref, l_ref, acc_ref)
        lam = _lambda(lq1_ref, lk1_ref, lq2_ref, lk2_ref, lam_init)
        acc = acc_ref[...]
        l = l_ref[...]
        for h in range(N_HEADS):
            r1 = slice(2 * h * SUBLANES, (2 * h + 1) * SUBLANES)
            r2 = slice((2 * h + 1) * SUBLANES, (2 * h + 2) * SUBLANES)
            cl = slice(h * HEAD_DIM, (h + 1) * HEAD_DIM)
            o_ref[:, cl] = _diff_combine(acc[r1, cl], l[r1], acc[r2, cl], l[r2], lam, gs_ref[...], lam_init)


def _fox_dec_kernel(pt_ref, q_ref, kn_ref, vn_ref, bn_ref, tri_ref, *rest, npg):
    k_refs = rest[:npg]
    v_refs = rest[npg:2 * npg]
    lf_refs = rest[2 * npg:3 * npg]
    o_ref, qr_ref, m_ref, l_ref, acc_ref, carry_ref = rest[3 * npg:]
    j = pl.program_id(1)

    @pl.when(j == 0)
    def _():
        _dec_init(q_ref, qr_ref, m_ref, l_ref, acc_ref, HEAD_DIM, N_HEADS)
        carry_ref[...] = jnp.zeros_like(carry_ref)

    terms = []
    for lf in lf_refs:
        terms += list(_split3(lf[...]))
    y = _dot(jnp.concatenate(terms, axis=0).astype(BF16), tri_ref[...])
    c = carry_ref[...]
    biases = []
    rows = BIAS_PARTS * SUBLANES
    for r in range(npg):
        yr = y[r * rows:(r + 1) * rows]
        blk = (yr[0:SUBLANES] + yr[SUBLANES:2 * SUBLANES]) + yr[2 * SUBLANES:3 * SUBLANES] + c
        biases.append(-blk)
        c = jnp.broadcast_to(blk[:, LANES - 1:LANES], blk.shape)
    carry_ref[...] = c

    qr = qr_ref[...]
    s = jnp.concatenate([_dot_nt(qr, _flat_heads(k)) for k in k_refs], axis=1)
    s = s + _expand_heads(jnp.concatenate(biases, axis=1), npg * LANES)
    _dec_update(s, [_flat_heads(v) for v in v_refs], m_ref, l_ref, acc_ref)

    @pl.when(j == pl.num_programs(1) - 1)
    def _():
        sn, vn = _new_tokens(qr, kn_ref, vn_ref, _expand_heads(bn_ref[...] - c, LANES))
        _dec_update(sn, [vn], m_ref, l_ref, acc_ref)
        acc = acc_ref[...]
        l = l_ref[...]
        for h in range(N_HEADS):
            rs = slice(h * SUBLANES, (h + 1) * SUBLANES)
            cl = slice(h * HEAD_DIM, (h + 1) * HEAD_DIM)
            o_ref[:, cl] = acc[rs, cl] / l[rs]


def _page_specs(layer, n_pages, npg, block):
    nz = (0,) * (len(block) - 2)

    def spec(r):
        return pl.BlockSpec(block, lambda b, j, pt, r=r: (layer, pt[b * n_pages + j * npg + r]) + nz)
    return [spec(r) for r in range(npg)]


def _dec_common(layer, tnew, rows):
    tok = pl.BlockSpec((tnew, ATT_W), lambda b, j, pt: (b, 0))
    new_kv = pl.BlockSpec((None, None, N_HEADS, tnew, HEAD_DIM), lambda b, j, pt: (layer, b, 0, 0, 0))
    scratch = [pltpu.VMEM((rows, ATT_W), BF16), pltpu.VMEM((rows, 1), F32), pltpu.VMEM((rows, 1), F32),
               pltpu.VMEM((rows, ATT_W), F32)]
    return tok, new_kv, scratch


def _diff_decode(pt, q, kn, vn, lq1, lk1, lq2, lk2, g_sub, cache_k, cache_v, layer, nb, n_pages, lam_init):
    tnew = q.shape[0] // nb
    assert tnew == SUBLANES
    page = cache_k.shape[3]
    npg = _tile(n_pages, DEC_PAGES_PER_STEP)
    tok, new_kv, scratch = _dec_common(layer, tnew, 2 * N_HEADS * SUBLANES)
    vec = pl.BlockSpec((1, DK_DIFF), lambda b, j, pt: (0, 0))
    pages = _page_specs(layer, n_pages, npg, (None, None, N_HEADS, page, HEAD_DIM))
    return pl.pallas_call(
        functools.partial(_diff_dec_kernel, npg=npg, lam_init=lam_init), name="diff_decode",
        grid_spec=pltpu.PrefetchScalarGridSpec(
            num_scalar_prefetch=1, grid=(nb, n_pages // npg),
            in_specs=[tok, new_kv, new_kv, vec, vec, vec, vec,
                      pl.BlockSpec((1, HEAD_DIM), lambda b, j, pt: (0, 0))] + pages + pages,
            out_specs=tok, scratch_shapes=scratch),
        out_shape=jax.ShapeDtypeStruct(q.shape, F32),
        compiler_params=_cparams(2, 4 * npg * math.prod(cache_k.shape[2:]) * 4 + (20 << 20)),
    )(pt, q, kn, vn, lq1, lk1, lq2, lk2, g_sub, *([cache_k] * npg), *([cache_v] * npg))


def _fox_decode(pt, q, kn, vn, bias_new, cache_k, cache_v, cache_lf, layer, nb, n_pages):
    tnew = q.shape[0] // nb
    assert tnew == SUBLANES
    page = cache_k.shape[3]
    assert page == LANES
    npg = _tile(n_pages, DEC_PAGES_PER_STEP)
    tok, new_kv, scratch = _dec_common(layer, tnew, N_HEADS * SUBLANES)
    pages = _page_specs(layer, n_pages, npg, (None, None, N_HEADS, page, HEAD_DIM))
    lf_pages = _page_specs(layer, n_pages, npg, (None, None, SUBLANES, page))
    idx = jnp.arange(page, dtype=jnp.int32)
    tri = (idx[:, None] <= idx[None, :]).astype(BF16)
    return pl.pallas_call(
        functools.partial(_fox_dec_kernel, npg=npg), name="fox_decode",
        grid_spec=pltpu.PrefetchScalarGridSpec(
            num_scalar_prefetch=1, grid=(nb, n_pages // npg),
            in_specs=[tok, new_kv, new_kv,
                      pl.BlockSpec((None, SUBLANES, LANES), lambda b, j, pt: (b, 0, 0)),
                      pl.BlockSpec((page, page), lambda b, j, pt: (0, 0))]
            + pages + pages + lf_pages,
            out_specs=tok, scratch_shapes=scratch + [pltpu.VMEM((SUBLANES, LANES), F32)]),
        out_shape=jax.ShapeDtypeStruct(q.shape, F32),
        compiler_params=_cparams(2, 4 * npg * math.prod(cache_k.shape[2:]) * 4 + (20 << 20)),
    )(pt, q, kn, vn, bias_new, tri, *([cache_k] * npg), *([cache_v] * npg), *([cache_lf] * npg))


def _out_proj_kernel(yc_ref, yd_ref, yf_ref, w_ref, x_ref, g1_ref, gn_ref, sc_ref, sh_ref, x1_ref, h2_ref):
    mix = jnp.concatenate([yc_ref[...].astype(BF16), yd_ref[...].astype(BF16), yf_ref[...].astype(BF16)], axis=1)
    x1 = x_ref[...] + g1_ref[...] * _dot(mix, w_ref[...])
    x1_ref[...] = x1
    h2_ref[...] = _norm_mod_value(x1, gn_ref[...], sc_ref[...], sh_ref[...]).astype(h2_ref.dtype)


def _out_proj(yc, yd, yf, w_out, layer, x, g1, gn, sc2, sh2, tm, tiles_per_group):
    m, d = x.shape
    row = lambda i: (i, 0)
    return pl.pallas_call(
        _out_proj_kernel, name="out_proj",
        grid=(m // tm,),
        in_specs=[pl.BlockSpec((tm, C_CONV), row), pl.BlockSpec((tm, ATT_W), row), pl.BlockSpec((tm, ATT_W), row),
                  _layer_spec(w_out, layer), pl.BlockSpec((tm, d), row), _mod_spec(g1, tiles_per_group),
                  _full_spec((1, d)), _mod_spec(sc2, tiles_per_group), _mod_spec(sh2, tiles_per_group)],
        out_specs=[pl.BlockSpec((tm, d), row), pl.BlockSpec((tm, d), row)],
        out_shape=[jax.ShapeDtypeStruct((m, d), F32), jax.ShapeDtypeStruct((m, d), BF16)],
        compiler_params=_cparams(1, 40 << 20),
    )(yc, yd, yf, w_out, x, g1, gn, sc2, sh2)


def _ffn_up_kernel(h_ref, wa_ref, wb_ref, g_ref):
    hb = h_ref[...]
    a = _dot(hb, wa_ref[...].astype(BF16))
    b = _dot(hb, wb_ref[...].astype(BF16))
    g_ref[...] = (a * jax.nn.sigmoid(a) * b).astype(g_ref.dtype)


def _ffn_up(h2, w_in, layer, tm, tf):
    m, d = h2.shape
    f = w_in.shape[2] // 2
    nf = f // tf
    return pl.pallas_call(
        _ffn_up_kernel, name="ffn_up",
        grid=(m // tm, nf),
        in_specs=[pl.BlockSpec((tm, d), lambda i, j: (i, 0)),
                  pl.BlockSpec((None, d, tf), lambda i, j: (layer, 0, j)),
                  pl.BlockSpec((None, d, tf), lambda i, j: (layer, 0, j + nf))],
        out_specs=pl.BlockSpec((tm, tf), lambda i, j: (i, j)),
        out_shape=jax.ShapeDtypeStruct((m, f), BF16),
        compiler_params=_cparams(2, 40 << 20),
    )(h2, w_in, w_in)


def _ffn_down_kernel(g_ref, w_ref, x_ref, g2_ref, *rest, next_norm):
    x2 = x_ref[...] + g2_ref[...] * _dot(g_ref[...], w_ref[...])
    if next_norm:
        gn_ref, sc_ref, sh_ref, x2_ref, hn_ref = rest
        hn_ref[...] = _norm_mod_value(x2, gn_ref[...], sc_ref[...], sh_ref[...]).astype(hn_ref.dtype)
    else:
        x2_ref, = rest
    x2_ref[...] = x2


def _ffn_down(g, w_out, layer, x1, g2, nxt, tm, tiles_per_group):
    m, f = g.shape
    d = w_out.shape[2]
    xspec = pl.BlockSpec((tm, d), lambda i: (i, 0))
    in_specs = [pl.BlockSpec((tm, f), lambda i: (i, 0)),
                pl.BlockSpec((None, f, d), lambda i: (layer, 0, 0), pipeline_mode=pl.Buffered(1)),
                xspec, _mod_spec(g2, tiles_per_group)]
    args = [g, w_out, x1, g2]
    out_specs, out_shape = [xspec], [jax.ShapeDtypeStruct((m, d), F32)]
    if nxt is not None:
        in_specs += [_full_spec((1, d)), _mod_spec(nxt[1], tiles_per_group), _mod_spec(nxt[2], tiles_per_group)]
        args += list(nxt)
        out_specs.append(xspec)
        out_shape.append(jax.ShapeDtypeStruct((m, d), BF16))
    out = pl.pallas_call(
        functools.partial(_ffn_down_kernel, next_norm=nxt is not None), name="ffn_down",
        grid=(m // tm,),
        in_specs=in_specs, out_specs=out_specs, out_shape=out_shape,
        compiler_params=_cparams(1, f * d * 2 + 6 * tm * (f + 4 * d) + (8 << 20)),
    )(*args)
    return (out[0], out[1]) if nxt is not None else (out[0], None)


def _rope_tables(pos):
    half = DK_DIFF // 2
    inv = ROPE_THETA ** (-jnp.arange(half, dtype=F32) / half)
    ang = pos.astype(F32)[:, None] * inv[None, :]
    cos = jnp.tile(jnp.cos(ang), (1, LANES // half))
    sin = jnp.sin(ang)
    sin = jnp.tile(jnp.concatenate([-sin, sin], axis=1), (1, LANES // DK_DIFF))
    return cos, sin


class _Group:
    def __init__(self, nb, t, is_prompt):
        self.nb, self.t, self.is_prompt = nb, t, is_prompt
        self.m = nb * t
        if is_prompt:
            self.tm = _tile(t, 256)
            self.tm_out = _tile(t, 512)
            self.tm_conv = _tile(t, 512)
            self.tm_ffn = _tile(t, 2048)
            self.tm_down = _tile(t, 256)
            self.tq = _tile(t, 512)
            self.down_tiles = t // self.tm_down
            self.out_tiles = t // self.tm_out
        else:
            self.tm = self.tm_out = self.tm_conv = self.tm_ffn = self.tm_down = self.m
            self.down_tiles = self.out_tiles = 1


def _mods(ada, grp):
    d = ada.shape[1] // 6
    parts = jnp.split(ada, 6, axis=-1)
    if grp.is_prompt:
        return [p.reshape(grp.nb, 1, d) for p in parts]
    return [jnp.repeat(p, grp.t, axis=0).reshape(1, grp.m, d) for p in parts]


def _layer(x, h, grp, wts, lw, layer, depth, mods, nxt, rope, lam_init, states, ctx):
    sh1, sc1, g1, sh2, sc2, g2 = mods
    cos, sin = rope
    nb, t = grp.nb, grp.t
    prompt = grp.is_prompt
    prev_d, prev_f = states[0:2], states[2:4]

    yc, conv_new = _proj_conv(h, wts["w_in"], layer, lw["conv_w"], lw["conv_b"], lw["conv_ln_g"],
                              lw["conv_ln_b"], None if prompt else ctx["state_pad"],
                              nb, t, grp.tm_conv, BF16 if prompt else F32)
    dres = _proj_diff(h, wts["w_in"], layer, depth, lw["diff_qn"], lw["diff_kn"],
                      cos, sin, prev_d, nb, t, grp.tm, prompt)
    fres = _proj_fox(h, wts["w_in"], wts["w_ff"], layer, depth, lw["fox_fb"],
                     lw["fox_qn"], lw["fox_kn"], prev_f, nb, t, grp.tm, prompt)
    qd, kd, vd = dres[:3]
    qf, kf, vf, lft, fbias = fres[:5]
    if prompt:
        yd = _diff_attn(qd, dres[3], dres[4], lw["lam_q1"], lw["lam_k1"], lw["lam_q2"], lw["lam_k2"],
                        lw["diff_subln"], nb, t, grp.tq, lam_init)
        yf = _fox_attn(qf, fres[5], fres[6], nb, t, grp.tq)
    else:
        yd = _diff_decode(ctx["pt"], qd, kd, vd, lw["lam_q1"], lw["lam_k1"], lw["lam_q2"], lw["lam_k2"],
                          lw["diff_subln"], ctx["cache_dk"], ctx["cache_dv"], layer, nb, ctx["n_pages"], lam_init)
        bn = jnp.transpose(fbias[0, :, :nb * t].reshape(SUBLANES, nb, t), (1, 0, 2))
        bn = jnp.pad(bn, ((0, 0), (0, 0), (0, LANES - t)))
        yf = _fox_decode(ctx["pt"], qf, kf, vf, bn, ctx["cache_fk"], ctx["cache_fv"], ctx["cache_lf"],
                         layer, nb, ctx["n_pages"])
    x1, h2 = _out_proj(yc, yd, yf, wts["w_out"], layer, x, g1, lw["norm_ffn"], sc2, sh2, grp.tm_out,
                       grp.out_tiles)
    g = _ffn_up(h2, wts["w_ffn_in"], layer, grp.tm_ffn, wts["tf"])
    x2, h_next = _ffn_down(g, wts["w_ffn_out"], layer, x1, g2, nxt, grp.tm_down, grp.down_tiles)
    return x2, h_next, (kd, vd, kf, vf), lft, conv_new


def kernel(x_prompt, x_sample, cache_diff_k, cache_diff_v, cache_fox_k, cache_fox_v, cache_fox_lf, state_conv,
           page_table, c_prompt, c_sample, norm_mix, norm_ffn, w_ada, b_ada, w_in, conv_w, conv_b, conv_ln_g,
           conv_ln_b, diff_qn, diff_kn, lam_q1, lam_k1, lam_q2, lam_k2, diff_subln, fox_qn, fox_kn, fox_fb,
           w_out, w_ffn_in, w_ffn_out):
    depth = w_in.shape[0]
    bp, tp, d = x_prompt.shape
    bs, ts, _ = x_sample.shape
    page = cache_diff_k.shape[2]
    n_pages = page_table.shape[1]
    d_ff = w_ffn_out.shape[1]
    assert d_ff % LANES == 0
    gp = _Group(bp, tp, True)
    gs = _Group(bs, ts, False)

    c_all = jnp.concatenate([c_prompt, c_sample], axis=0)
    n_c = c_all.shape[0]
    c_all = jnp.pad(c_all, ((0, (-n_c) % (2 * SUBLANES)), (0, 0)))
    ada = _ada(c_all, w_ada, b_ada)
    mods_p = [_mods(ada[l, :bp], gp) for l in range(depth)]
    mods_s = [_mods(ada[l, bp:bp + bs], gs) for l in range(depth)]

    n_main = FOX_COL0 + 3 * ATT_W
    wts = {"w_in": w_in.astype(BF16)}
    wts["w_ff"] = jnp.pad(w_in[:, :, n_main:], ((0, 0), (0, 0), (0, LANES - N_HEADS))).astype(BF16)
    wts["w_out"] = w_out.astype(BF16)
    wts["w_ffn_in"] = w_ffn_in
    wts["w_ffn_out"] = w_ffn_out.astype(BF16)
    wts["tf"] = _tile(d_ff, 256)

    rope_p = _rope_tables(jnp.arange(tp, dtype=jnp.int32))
    cos_s, sin_s = _rope_tables(n_pages * page + jnp.arange(ts, dtype=jnp.int32))
    rope_s = (jnp.tile(cos_s, (bs, 1)), jnp.tile(sin_s, (bs, 1)))

    head_major = lambda c: jnp.transpose(c, (0, 1, 3, 2, 4))
    ctx = dict(pt=page_table.reshape(-1).astype(jnp.int32), n_pages=n_pages,
               cache_dk=head_major(cache_diff_k), cache_dv=head_major(cache_diff_v),
               cache_fk=head_major(cache_fox_k), cache_fv=head_major(cache_fox_v),
               cache_lf=jnp.pad(jnp.transpose(cache_fox_lf, (0, 1, 3, 2)),
                                ((0, 0), (0, 0), (0, SUBLANES - N_HEADS), (0, 0))))
    state_pad = jnp.pad(state_conv, ((0, 0), (0, 0), (HIST_PAD - CONV_HIST, 0), (0, 0)))

    xp = x_prompt.reshape(bp * tp, d)
    xs = x_sample.reshape(bs * ts, d)
    g0 = norm_mix[0][None, :]
    hp = _norm_mod(xp, g0, mods_p[0][1], mods_p[0][0], gp.tm_down, gp.down_tiles)
    hs = _norm_mod(xs, g0, mods_s[0][1], mods_s[0][0], gs.tm, 1)
    kv_p = tuple(jnp.zeros((depth, bp, N_HEADS, tp, HEAD_DIM), F32) for _ in range(4))
    kv_s = tuple(jnp.zeros((depth, bs, N_HEADS, ts, HEAD_DIM), F32) for _ in range(4))
    lf_p, lf_s, conv_p, conv_s = [], [], [], []
    for l in range(depth):
        lam_init = 0.8 - 0.6 * math.exp(-0.3 * l)
        lw = dict(
            conv_w=conv_w[l], conv_b=conv_b[l][None, :], conv_ln_g=conv_ln_g[l][None, :],
            conv_ln_b=conv_ln_b[l][None, :],
            diff_qn=jnp.tile(diff_qn[l], LANES // DK_DIFF)[None, :],
            diff_kn=jnp.tile(diff_kn[l], LANES // DK_DIFF)[None, :],
            lam_q1=lam_q1[l][None, :], lam_k1=lam_k1[l][None, :], lam_q2=lam_q2[l][None, :],
            lam_k2=lam_k2[l][None, :], diff_subln=diff_subln[l][None, :],
            fox_qn=fox_qn[l][None, :], fox_kn=fox_kn[l][None, :],
            fox_fb=jnp.pad(fox_fb[l], (0, LANES - N_HEADS))[None, :],
            norm_ffn=norm_ffn[l][None, :])
        last = l == depth - 1
        nxt_p = None if last else (norm_mix[l + 1][None, :], mods_p[l + 1][1], mods_p[l + 1][0])
        nxt_s = None if last else (norm_mix[l + 1][None, :], mods_s[l + 1][1], mods_s[l + 1][0])
        xp, hp, kv_p, lft, cn = _layer(xp, hp, gp, wts, lw, l, depth, mods_p[l], nxt_p, rope_p, lam_init, kv_p,
                                       None)
        lf_p.append(lft)
        conv_p.append(cn)
        xs, hs, kv_s, lft, cn = _layer(xs, hs, gs, wts, lw, l, depth, mods_s[l], nxt_s, rope_s, lam_init, kv_s,
                                       dict(ctx, state_pad=state_pad[l]))
        lf_s.append(lft)
        conv_s.append(cn)

    kv = lambda a: jnp.transpose(a, (0, 1, 3, 2, 4))
    lf_p = jnp.transpose(jnp.stack(lf_p, axis=0)[:, :, :N_HEADS, :], (0, 1, 3, 2))
    lf_s = jnp.stack(lf_s, axis=0)[:, 0, :N_HEADS, :bs * ts]
    lf_s = jnp.transpose(lf_s.reshape(depth, N_HEADS, bs, ts), (0, 2, 3, 1))
    return (xp.reshape(bp, tp, d), xs.reshape(bs, ts, d),
            kv(kv_p[0]), kv(kv_p[1]), kv(kv_p[2]), kv(kv_p[3]), lf_p, jnp.stack(conv_p, axis=0),
            kv(kv_s[0]), kv(kv_s[1]), kv(kv_s[2]), kv(kv_s[3]), lf_s, jnp.stack(conv_s, axis=0))
```
